```python
import jax, jax.numpy as jnp
from jax import lax
import numpy as np

D_MODEL = 1024
BATCH = 2
SEQ = 8192
DEPTH = 4
DEC_BATCH = 128
DEC_SEQ = 1
PAST_LEN = 8192
PAGE_SIZE = 128

N_HEADS = 16
HEAD_DIM = D_MODEL // N_HEADS
N_KV_HEADS = 4
GROUP = N_HEADS // N_KV_HEADS
WINDOW = 128
BLOCK = WINDOW
CONV_WIDTH = 31
D_FF = 4 * D_MODEL
N_A_LAYERS = DEPTH // 2
N_B_LAYERS = DEPTH - N_A_LAYERS
RMS_EPS = 1e-6
LN_EPS = 1e-5

kernel_name = 'yoco_conformer_conv_swa_sink_decoder_step'


def rmsnorm(x, g):
    xf = x.astype(jnp.float32)
    y = xf * lax.rsqrt(jnp.mean(jnp.square(xf), axis=-1, keepdims=True) + RMS_EPS)
    return (y * g.astype(jnp.float32)).astype(x.dtype)


def layernorm(x, g, b):
    xf = x.astype(jnp.float32)
    mu = jnp.mean(xf, axis=-1, keepdims=True)
    var = jnp.mean(jnp.square(xf - mu), axis=-1, keepdims=True)
    y = (xf - mu) * lax.rsqrt(var + LN_EPS)
    return (y * g.astype(jnp.float32) + b.astype(jnp.float32)).astype(x.dtype)


def alibi_slopes():
    return jnp.exp2(-8.0 * jnp.arange(1, N_HEADS + 1, dtype=jnp.float32) / N_HEADS)


def causal_dwconv(u_ext, w, b):
    out = lax.conv_general_dilated(
        u_ext, w[:, None, :].astype(u_ext.dtype), window_strides=(1,), padding='VALID',
        dimension_numbers=('NWC', 'WIO', 'NWC'), feature_group_count=u_ext.shape[-1])
    return out + b


def conv_module(h, prev, w1, b1, dw, dwb, ln_g, ln_b, w2, b2):
    a = h @ w1 + b1
    u = a[..., :D_MODEL] * jax.nn.sigmoid(a[..., D_MODEL:])
    u_ext = jnp.concatenate([prev.astype(u.dtype), u], axis=1)
    c = causal_dwconv(u_ext, dw, dwb)
    c = jax.nn.silu(layernorm(c, ln_g, ln_b))
    return c @ w2 + b2, u_ext[:, -(CONV_WIDTH - 1):]


def sq_relu_mlp(h, w1, w2):
    return jnp.square(jax.nn.relu(h @ w1)) @ w2


def shared_kv(x, g, w_kv, k_norm_g):
    h = rmsnorm(x, g)
    kv = h @ w_kv
    b, t = x.shape[0], x.shape[1]
    k = kv[..., :N_KV_HEADS * HEAD_DIM].reshape(b, t, N_KV_HEADS, HEAD_DIM)
    v = kv[..., N_KV_HEADS * HEAD_DIM:].reshape(b, t, N_KV_HEADS, HEAD_DIM)
    return rmsnorm(k, k_norm_g), v


def alibi_bias(delta):
    return -alibi_slopes().reshape(N_KV_HEADS, GROUP, 1, 1) * delta.astype(jnp.float32)[None, None]


def sink_attend(q, k, v, alibi, mask, sinks):
    s = jnp.einsum('bnqkgd,bnskd->bnkgqs', q, k, preferred_element_type=jnp.float32) * (HEAD_DIM ** -0.5)
    s = jnp.where(mask[None, :, None, None], s + alibi, -jnp.inf)
    sink = sinks.astype(jnp.float32).reshape(N_KV_HEADS, GROUP, 1, 1)
    m = jnp.maximum(jnp.max(s, axis=-1, keepdims=True), sink)
    p = jnp.exp(s - m)
    denom = jnp.sum(p, axis=-1, keepdims=True) + jnp.exp(sink - m)
    return jnp.einsum('bnkgqs,bnskd->bnqkgd', p / denom, v.astype(jnp.float32))


def window_attention(h, k_ctx, v_ctx, alibi, mask, wq, q_g, sinks, wo):
    b, t = h.shape[0], h.shape[1]
    n = k_ctx.shape[1]
    q = rmsnorm((h @ wq).reshape(b, t, N_HEADS, HEAD_DIM), q_g)
    q = q.reshape(b, n, t // n, N_KV_HEADS, GROUP, HEAD_DIM)
    o = sink_attend(q, k_ctx, v_ctx, alibi, mask, sinks)
    return o.reshape(b, t, N_HEADS * HEAD_DIM).astype(h.dtype) @ wo


def setup_inputs(seed: int = 0) -> dict:
    key = jax.random.key(seed)
    ks = jax.random.split(key, 32)
    f32 = jnp.float32

    def nrm(k, shape, scale):
        return scale * jax.random.normal(k, shape, f32)

    buf = min(WINDOW, PAST_LEN)
    D = D_MODEL
    return {
        'x_prompt': nrm(ks[0], (BATCH, SEQ, D), 1.0),
        'x_sample': nrm(ks[1], (DEC_BATCH, DEC_SEQ, D), 1.0),
        'state_conv': nrm(ks[2], (N_A_LAYERS, DEC_BATCH, CONV_WIDTH - 1, D), 0.5),
        'state_win_k': nrm(ks[3], (DEC_BATCH, buf, N_KV_HEADS, HEAD_DIM), 1.0),
        'state_win_v': nrm(ks[4], (DEC_BATCH, buf, N_KV_HEADS, HEAD_DIM), 1.0),
        'norm_mix_g': 1.0 + nrm(ks[5], (DEPTH, D), 0.02),
        'norm_mlp_g': 1.0 + nrm(ks[6], (DEPTH, D), 0.02),
        'conv_w1': nrm(ks[7], (N_A_LAYERS, D, 2 * D), D ** -0.5),
        'conv_b1': nrm(ks[8], (N_A_LAYERS, 2 * D), 0.02),
        'conv_dw': nrm(ks[9], (N_A_LAYERS, CONV_WIDTH, D), CONV_WIDTH ** -0.5),
        'conv_dwb': nrm(ks[10], (N_A_LAYERS, D), 0.02),
        'conv_ln_g': 1.0 + nrm(ks[11], (N_A_LAYERS, D), 0.02),
        'conv_ln_b': nrm(ks[12], (N_A_LAYERS, D), 0.02),
        'conv_w2': nrm(ks[13], (N_A_LAYERS, D, D), D ** -0.5),
        'conv_b2': nrm(ks[14], (N_A_LAYERS, D), 0.02),
        'kv_norm_g': 1.0 + nrm(ks[15], (D,), 0.02),
        'w_kv': nrm(ks[16], (D, 2 * N_KV_HEADS * HEAD_DIM), D ** -0.5),
        'k_norm_g': 1.0 + nrm(ks[17], (HEAD_DIM,), 0.02),
        'attn_wq': nrm(ks[18], (N_B_LAYERS, D, N_HEADS * HEAD_DIM), D ** -0.5),
        'q_norm_g': 1.0 + nrm(ks[19], (N_B_LAYERS, HEAD_DIM), 0.02),
        'attn_sinks': nrm(ks[20], (N_B_LAYERS, N_HEADS), 0.5),
        'attn_wo': nrm(ks[21], (N_B_LAYERS, N_HEADS * HEAD_DIM, D), (N_HEADS * HEAD_DIM) ** -0.5),
        'mlp_w1': nrm(ks[22], (DEPTH, D, D_FF), D ** -0.5),
        'mlp_w2': nrm(ks[23], (DEPTH, D_FF, D), D_FF ** -0.5),
    }


def reference(x_prompt, x_sample, state_conv, state_win_k, state_win_v,
              norm_mix_g, norm_mlp_g,
              conv_w1, conv_b1, conv_dw, conv_dwb, conv_ln_g, conv_ln_b, conv_w2, conv_b2,
              kv_norm_g, w_kv, k_norm_g,
              attn_wq, q_norm_g, attn_sinks, attn_wo,
              mlp_w1, mlp_w2):
    xp, xs = x_prompt, x_sample
    bp, sp = xp.shape[0], xp.shape[1]
    bs, ts = xs.shape[0], xs.shape[1]
    n_blk = sp // BLOCK

    qi = jnp.arange(BLOCK)[:, None]
    kj = jnp.arange(2 * BLOCK)[None, :]
    delta_p = BLOCK + qi - kj
    band_p = (delta_p >= 0) & (delta_p <= WINDOW)
    mask_p = band_p[None] & ((jnp.arange(n_blk)[:, None, None] > 0) | (kj[None] >= BLOCK))
    alibi_p = alibi_bias(delta_p)

    buf_len = state_win_k.shape[1]
    si = jnp.arange(ts)[:, None]
    sj = jnp.arange(buf_len + ts)[None, :]
    delta_s = buf_len + si - sj
    mask_s = ((delta_s >= 0) & (delta_s <= WINDOW))[None]
    alibi_s = alibi_bias(delta_s)

    conv_prev_p = jnp.zeros((bp, CONV_WIDTH - 1, D_MODEL), xp.dtype)
    conv_new_p, conv_new_s = [], []
    kp_ctx = vp_ctx = ks_ctx = vs_ctx = None
    new_kp = new_vp = new_ks = new_vs = None

    for l in range(DEPTH):
        if l < N_A_LAYERS:
            a = l
            cw = (conv_w1[a], conv_b1[a], conv_dw[a], conv_dwb[a], conv_ln_g[a], conv_ln_b[a], conv_w2[a], conv_b2[a])
            yp, cp = conv_module(rmsnorm(xp, norm_mix_g[l]), conv_prev_p, *cw)
            ys, cs = conv_module(rmsnorm(xs, norm_mix_g[l]), state_conv[a], *cw)
            xp, xs = xp + yp, xs + ys
            conv_new_p.append(cp)
            conv_new_s.append(cs)
        else:
            if l == N_A_LAYERS:
                kp, vp = shared_kv(xp, kv_norm_g, w_kv, k_norm_g)
                kb = kp.reshape(bp, n_blk, BLOCK, N_KV_HEADS, HEAD_DIM)
                vb = vp.reshape(bp, n_blk, BLOCK, N_KV_HEADS, HEAD_DIM)
                shift = ((0, 0), (1, 0), (0, 0), (0, 0), (0, 0))
                kp_ctx = jnp.concatenate([jnp.pad(kb[:, :-1], shift), kb], axis=2)
                vp_ctx = jnp.concatenate([jnp.pad(vb[:, :-1], shift), vb], axis=2)
                keep_p = min(WINDOW, sp)
                new_kp, new_vp = kp[:, -keep_p:], vp[:, -keep_p:]

                kn, vn = shared_kv(xs, kv_norm_g, w_kv, k_norm_g)
                k_all = jnp.concatenate([state_win_k.astype(kn.dtype), kn], axis=1)
                v_all = jnp.concatenate([state_win_v.astype(vn.dtype), vn], axis=1)
                ks_ctx, vs_ctx = k_all[:, None], v_all[:, None]
                new_ks, new_vs = k_all[:, -buf_len:], v_all[:, -buf_len:]
            bl = l - N_A_LAYERS
            aw = (attn_wq[bl], q_norm_g[bl], attn_sinks[bl], attn_wo[bl])
            xp = xp + window_attention(rmsnorm(xp, norm_mix_g[l]), kp_ctx, vp_ctx, alibi_p, mask_p, *aw)
            xs = xs + window_attention(rmsnorm(xs, norm_mix_g[l]), ks_ctx, vs_ctx, alibi_s, mask_s, *aw)
        xp = xp + sq_relu_mlp(rmsnorm(xp, norm_mlp_g[l]), mlp_w1[l], mlp_w2[l])
        xs = xs + sq_relu_mlp(rmsnorm(xs, norm_mlp_g[l]), mlp_w1[l], mlp_w2[l])

    new_conv_p = jnp.stack(conv_new_p, axis=0)
    new_conv_s = jnp.stack(conv_new_s, axis=0)
    return (xp, xs, new_conv_p, new_conv_s, new_kp, new_vp, new_ks, new_vs)
```

```python
import functools

import jax
import jax.numpy as jnp
from jax import lax
from jax.experimental import pallas as pl
from jax.experimental.pallas import tpu as pltpu

F32 = jnp.float32
BF16 = jnp.bfloat16

D_MODEL = 1024
N_HEADS = 16
HEAD_DIM = 64
N_KV_HEADS = 4
GROUP = 4
KV_DIM = N_KV_HEADS * HEAD_DIM
WINDOW = 128
BLOCK = 128
CONV_WIDTH = 31
CONV_PREV = CONV_WIDTH - 1
CARRY = 32
D_FF = 4 * D_MODEL
RMS_EPS = 1e-6
LN_EPS = 1e-5

TM = 512
FF_CHUNK = 1024
CONV_ROWS = 64
LANES = 128
SAMPLE_BLOCK = 32
VMEM_LIMIT = 56 * 1024 * 1024


def _const_spec(shape):
    nd = len(shape)
    return pl.BlockSpec(shape, lambda *_: (0,) * nd, pipeline_mode=pl.Buffered(1))


def _params(sem="arbitrary"):
    return pltpu.CompilerParams(dimension_semantics=(sem,), vmem_limit_bytes=VMEM_LIMIT)


def _rms(x, g):
    return x * lax.rsqrt(jnp.mean(x * x, axis=-1, keepdims=True) + RMS_EPS) * g


def _dot(a, b):
    return jnp.dot(a, b, preferred_element_type=F32)


def _dot_nt(a, b):
    return lax.dot_general(a, b, (((1,), (1,)), ((), ())), preferred_element_type=F32)


def _segment_ones():
    r = lax.broadcasted_iota(jnp.int32, (KV_DIM, KV_DIM), 0) // HEAD_DIM
    c = lax.broadcasted_iota(jnp.int32, (KV_DIM, KV_DIM), 1) // HEAD_DIM
    return (r == c).astype(BF16)


def _head_rms(z, g, seg):
    s = z * z
    hi = s.astype(BF16)
    lo = (s - hi.astype(F32)).astype(BF16)
    ms = (_dot(hi, seg) + _dot(lo, seg)) * (1.0 / HEAD_DIM)
    return z * lax.rsqrt(ms + RMS_EPS) * g


def _mlp_kernel(x_ref, g_ref, w1_ref, w2_ref, o_ref):
    x = x_ref[...]
    h = _rms(x, g_ref[...]).astype(BF16)
    acc = x
    for c in range(D_FF // FF_CHUNK):
        a = _dot(h, w1_ref[:, c * FF_CHUNK:(c + 1) * FF_CHUNK])
        a = jnp.maximum(a, 0.0)
        a = (a * a).astype(BF16)
        acc = acc + _dot(a, w2_ref[c * FF_CHUNK:(c + 1) * FF_CHUNK, :])
    o_ref[...] = acc


def _mlp(x, g, w1, w2, tm):
    rows = x.shape[0]
    return pl.pallas_call(
        _mlp_kernel,
        grid=(rows // tm,),
        in_specs=[pl.BlockSpec((tm, D_MODEL), lambda i: (i, 0)),
                  _const_spec((1, D_MODEL)),
                  _const_spec((D_MODEL, D_FF)),
                  _const_spec((D_FF, D_MODEL))],
        out_specs=pl.BlockSpec((tm, D_MODEL), lambda i: (i, 0)),
        out_shape=jax.ShapeDtypeStruct((rows, D_MODEL), F32),
        compiler_params=_params(),
        name="mlp",
    )(x, g, w1, w2)


def _glu_in(x, g, w1_ref, b1_ref):
    h = _rms(x, g).astype(BF16)
    a = _dot(h, w1_ref[...]) + b1_ref[...]
    return a[:, :D_MODEL] * jax.nn.sigmoid(a[:, D_MODEL:])


def _conv_out(x, c, lng, lnb, w2_ref, b2_ref):
    mu = jnp.mean(c, axis=-1, keepdims=True)
    d = c - mu
    var = jnp.mean(d * d, axis=-1, keepdims=True)
    y = d * lax.rsqrt(var + LN_EPS) * lng + lnb
    y = y * jax.nn.sigmoid(y)
    return x + _dot(y.astype(BF16), w2_ref[...]) + b2_ref[...]


def _amix_prompt_kernel(x_ref, g_ref, w1_ref, b1_ref, dw_ref, dwb_ref, lng_ref, lnb_ref, w2_ref, b2_ref,
                        o_ref, carry_ref, ubuf, cbuf, *, nt):
    t = pl.program_id(0) % nt

    @pl.when(t == 0)
    def _():
        ubuf[0:CARRY, :] = jnp.zeros((CARRY, D_MODEL), F32)

    x = x_ref[...]
    ubuf[CARRY:CARRY + TM, :] = _glu_in(x, g_ref[...], w1_ref, b1_ref)

    first = CARRY - CONV_PREV

    def conv_cols(ci, carry):
        c0 = pl.multiple_of(ci * LANES, LANES)
        for r in range(TM // CONV_ROWS):
            r0 = r * CONV_ROWS
            acc = jnp.broadcast_to(dwb_ref[:, pl.ds(c0, LANES)], (CONV_ROWS, LANES))
            for j in range(CONV_WIDTH):
                w = dw_ref[j:j + 1, pl.ds(c0, LANES)]
                acc = acc + ubuf[r0 + first + j:r0 + first + j + CONV_ROWS, pl.ds(c0, LANES)] * w
            cbuf[r0:r0 + CONV_ROWS, pl.ds(c0, LANES)] = acc
        return carry

    lax.fori_loop(0, D_MODEL // LANES, conv_cols, 0)

    tail = ubuf[TM:TM + CARRY, :]
    carry_ref[0] = tail
    ubuf[0:CARRY, :] = tail

    o_ref[...] = _conv_out(x, cbuf[...], lng_ref[...], lnb_ref[...], w2_ref, b2_ref)


def _amix_prompt(x, batch, g, w1, b1, dw, dwb, lng, lnb, w2, b2):
    rows = x.shape[0]
    nt = rows // batch // TM
    return pl.pallas_call(
        functools.partial(_amix_prompt_kernel, nt=nt),
        grid=(rows // TM,),
        in_specs=[pl.BlockSpec((TM, D_MODEL), lambda i: (i, 0)),
                  _const_spec((1, D_MODEL)),
                  _const_spec((D_MODEL, 2 * D_MODEL)),
                  _const_spec((1, 2 * D_MODEL)),
                  _const_spec((CONV_WIDTH, D_MODEL)),
                  _const_spec((1, D_MODEL)),
                  _const_spec((1, D_MODEL)),
                  _const_spec((1, D_MODEL)),
                  _const_spec((D_MODEL, D_MODEL)),
                  _const_spec((1, D_MODEL))],
        out_specs=[pl.BlockSpec((TM, D_MODEL), lambda i: (i, 0)),
                   pl.BlockSpec((1, CARRY, D_MODEL), lambda i: (i // nt, 0, 0))],
        out_shape=[jax.ShapeDtypeStruct((rows, D_MODEL), F32),
                   jax.ShapeDtypeStruct((batch, CARRY, D_MODEL), F32)],
        scratch_shapes=[pltpu.VMEM((CARRY + TM, D_MODEL), F32),
                        pltpu.VMEM((TM, D_MODEL), F32)],
        compiler_params=_params(),
        name="amix_prompt",
    )(x, g, w1, b1, dw, dwb, lng, lnb, w2, b2)


def _amix_sample_kernel(x_ref, st_ref, g_ref, w1_ref, b1_ref, dw_ref, dwb_ref, lng_ref, lnb_ref, w2_ref, b2_ref,
                        o_ref, st_out_ref, u_sc, c_sc, *, nsteps):
    i = pl.program_id(0)

    @pl.when(i == 0)
    def _():
        u_sc[...] = _glu_in(x_ref[...], g_ref[...], w1_ref, b1_ref)

    r0 = pl.multiple_of(i * SAMPLE_BLOCK, SAMPLE_BLOCK)
    u = u_sc[pl.ds(r0, SAMPLE_BLOCK), :]
    acc = dwb_ref[...] + u * dw_ref[CONV_PREV:CONV_WIDTH, :]
    for j in range(CONV_PREV):
        row = st_ref[:, j, :]
        acc = acc + row * dw_ref[j:j + 1, :]
        if j > 0:
            st_out_ref[:, j - 1, :] = row
    st_out_ref[:, CONV_PREV - 1, :] = u
    c_sc[pl.ds(r0, SAMPLE_BLOCK), :] = acc

    @pl.when(i == nsteps - 1)
    def _():
        o_ref[...] = _conv_out(x_ref[...], c_sc[...], lng_ref[...], lnb_ref[...], w2_ref, b2_ref)


def _amix_sample(x, state, g, w1, b1, dw, dwb, lng, lnb, w2, b2):
    n = x.shape[0]
    nsteps = n // SAMPLE_BLOCK
    st_spec = pl.BlockSpec((SAMPLE_BLOCK, CONV_PREV, D_MODEL), lambda i: (i, 0, 0))
    return pl.pallas_call(
        functools.partial(_amix_sample_kernel, nsteps=nsteps),
        grid=(nsteps,),
        in_specs=[_const_spec((n, D_MODEL)),
                  st_spec,
                  _const_spec((1, D_MODEL)),
                  _const_spec((D_MODEL, 2 * D_MODEL)),
                  _const_spec((1, 2 * D_MODEL)),
                  _const_spec((CONV_WIDTH, D_MODEL)),
                  _const_spec((1, D_MODEL)),
                  _const_spec((1, D_MODEL)),
                  _const_spec((1, D_MODEL)),
                  _const_spec((D_MODEL, D_MODEL)),
                  _const_spec((1, D_MODEL))],
        out_specs=[pl.BlockSpec((n, D_MODEL), lambda i: (0, 0)),
                   st_spec],
        out_shape=[jax.ShapeDtypeStruct((n, D_MODEL), F32),
                   jax.ShapeDtypeStruct(state.shape, F32)],
        scratch_shapes=[pltpu.VMEM((n, D_MODEL), F32),
                        pltpu.VMEM((n, D_MODEL), F32)],
        compiler_params=_params(),
        name="amix_sample",
    )(x, state, g, w1, b1, dw, dwb, lng, lnb, w2, b2)


def _kv_kernel(x_ref, g_ref, w_ref, kg_ref, kb_ref, vb_ref, kf_ref, vf_ref, *, keep):
    h = _rms(x_ref[...], g_ref[...]).astype(BF16)
    kv = _dot(h, w_ref[...])
    k = _head_rms(kv[:, :KV_DIM], kg_ref[...], _segment_ones())
    v = kv[:, KV_DIM:]
    kb_ref[...] = k.astype(BF16)
    vb_ref[...] = v.astype(BF16)
    tm = k.shape[0]
    kf_ref[...] = k[tm - keep:, :]
    vf_ref[...] = v[tm - keep:, :]


def _kv_proj(x, groups, g, w, kg, tm, keep):
    rows = x.shape[0]
    nt = rows // groups // tm
    row_spec = pl.BlockSpec((tm, KV_DIM), lambda i: (i, 0))
    keep_spec = pl.BlockSpec((keep, KV_DIM), lambda i: (i // nt, 0))
    return pl.pallas_call(
        functools.partial(_kv_kernel, keep=keep),
        grid=(rows // tm,),
        in_specs=[pl.BlockSpec((tm, D_MODEL), lambda i: (i, 0)),
                  _const_spec((1, D_MODEL)),
                  _const_spec((D_MODEL, 2 * KV_DIM)),
                  _const_spec((1, KV_DIM))],
        out_specs=[row_spec, row_spec, keep_spec, keep_spec],
        out_shape=[jax.ShapeDtypeStruct((rows, KV_DIM), BF16),
                   jax.ShapeDtypeStruct((rows, KV_DIM), BF16),
                   jax.ShapeDtypeStruct((groups * keep, KV_DIM), F32),
                   jax.ShapeDtypeStruct((groups * keep, KV_DIM), F32)],
        compiler_params=_params(),
        name="kv_proj",
    )(x, g, w, kg)


def _queries(x, g, wq_ref, qg, seg):
    h = _rms(x, g).astype(BF16)
    q = _dot(h, wq_ref[...])
    out = []
    for gi in range(GROUP):
        qn = _head_rms(q[:, gi * KV_DIM:(gi + 1) * KV_DIM], qg, seg)
        out.append(qn * (HEAD_DIM ** -0.5))
    return out


def _lane_segment(k, rows):
    lane = lax.broadcasted_iota(jnp.int32, (rows, KV_DIM), 1)
    return (lane >= k * HEAD_DIM) & (lane < (k + 1) * HEAD_DIM)


def _attn_prompt_kernel(slopes_ref, sinks_ref, x_ref, kp_ref, kc_ref, vp_ref, vc_ref, g_ref, wq_ref, qg_ref, wo_ref,
                        o_ref, bias_sc, attn_sc, *, nt):
    i = pl.program_id(0)
    t = i % nt
    delta = (BLOCK + lax.broadcasted_iota(jnp.int32, (BLOCK, 2 * BLOCK), 0)
             - lax.broadcasted_iota(jnp.int32, (BLOCK, 2 * BLOCK), 1))
    band = (delta >= 0) & (delta <= WINDOW)

    @pl.when(i == 0)
    def _():
        df = delta.astype(F32)
        for hd in range(N_HEADS):
            bias_sc[hd] = jnp.where(band, -slopes_ref[hd] * df, -jnp.inf)

    x = x_ref[...]
    seg = _segment_ones()
    qs = _queries(x, g_ref[...], wq_ref, qg_ref[...], seg)
    kcur = kc_ref[...]
    vcur = vc_ref[...]
    has_prev = t > 0
    col = lax.broadcasted_iota(jnp.int32, (GROUP * BLOCK, 2 * BLOCK), 1)
    no_prev = jnp.where((col < BLOCK) & jnp.logical_not(has_prev), -jnp.inf, 0.0)

    for qb in range(TM // BLOCK):
        rows = slice(qb * BLOCK, (qb + 1) * BLOCK)
        if qb == 0:
            kctx = jnp.concatenate([kp_ref[...], kcur[:BLOCK]], axis=0)
            vctx = jnp.concatenate([vp_ref[...], vcur[:BLOCK]], axis=0)
        else:
            kctx = kcur[(qb - 1) * BLOCK:(qb + 1) * BLOCK]
            vctx = vcur[(qb - 1) * BLOCK:(qb + 1) * BLOCK]
        for k in range(N_KV_HEADS):
            seg_k = _lane_segment(k, BLOCK)
            qm = jnp.concatenate([jnp.where(seg_k, qs[gi][rows], 0.0) for gi in range(GROUP)], axis=0)
            s = _dot_nt(qm.astype(BF16), kctx)
            s = s + bias_sc[k * GROUP:(k + 1) * GROUP].reshape(GROUP * BLOCK, 2 * BLOCK)
            if qb == 0:
                s = s + no_prev
            ps, inv = [], []
            for gi in range(GROUP):
                sg = s[gi * BLOCK:(gi + 1) * BLOCK]
                sink = sinks_ref[k * GROUP + gi]
                m = jnp.maximum(jnp.max(sg, axis=-1, keepdims=True), sink)
                p = jnp.exp(sg - m)
                inv.append(1.0 / (jnp.sum(p, axis=-1, keepdims=True) + jnp.exp(sink - m)))
                ps.append(p.astype(BF16))
            pv = _dot(jnp.concatenate(ps, axis=0), vctx)
            for gi in range(GROUP):
                og = pv[gi * BLOCK:(gi + 1) * BLOCK] * inv[gi]
                lanes = slice(gi * KV_DIM + k * HEAD_DIM, gi * KV_DIM + (k + 1) * HEAD_DIM)
                attn_sc[rows, lanes] = og[:, k * HEAD_DIM:(k + 1) * HEAD_DIM]

    o_ref[...] = x + _dot(attn_sc[...].astype(BF16), wo_ref[...])


def _attn_prompt(x, batch, kb, vb, slopes, sinks, g, wq, qg, wo):
    rows = x.shape[0]
    nt = rows // batch // TM
    per = TM // BLOCK
    prev_spec = pl.BlockSpec((BLOCK, KV_DIM), lambda i: (jnp.maximum(i * per - 1, 0), 0))
    cur_spec = pl.BlockSpec((TM, KV_DIM), lambda i: (i, 0))
    smem = pl.BlockSpec(memory_space=pltpu.SMEM)
    return pl.pallas_call(
        functools.partial(_attn_prompt_kernel, nt=nt),
        grid=(rows // TM,),
        in_specs=[smem, smem,
                  pl.BlockSpec((TM, D_MODEL), lambda i: (i, 0)),
                  prev_spec, cur_spec, prev_spec, cur_spec,
                  _const_spec((1, D_MODEL)),
                  _const_spec((D_MODEL, D_MODEL)),
                  _const_spec((1, KV_DIM)),
                  _const_spec((D_MODEL, D_MODEL))],
        out_specs=pl.BlockSpec((TM, D_MODEL), lambda i: (i, 0)),
        out_shape=jax.ShapeDtypeStruct((rows, D_MODEL), F32),
        scratch_shapes=[pltpu.VMEM((N_HEADS, BLOCK, 2 * BLOCK), F32),
                        pltpu.VMEM((TM, D_MODEL), F32)],
        compiler_params=_params(),
        name="attn_prompt",
    )(slopes, sinks, x, kb, kb, vb, vb, g, wq, qg, wo)


def _attn_sample_kernel(slopes_ref, sinks_ref, x_ref, ks_ref, vs_ref, kn_ref, vn_ref, g_ref, wq_ref, qg_ref, wo_ref,
                        o_ref, qx_sc, ox_sc, *, nsteps, buf_len):
    i = pl.program_id(0)
    n = x_ref.shape[0]

    @pl.when(i == 0)
    def _():
        qs = _queries(x_ref[...], g_ref[...], wq_ref, qg_ref[...], _segment_ones())
        for k in range(N_KV_HEADS):
            seg_k = _lane_segment(k, n)
            for gi in range(GROUP):
                qx_sc[:, k * GROUP + gi, :] = jnp.where(seg_k, qs[gi], 0.0)

    hrow = lax.broadcasted_iota(jnp.int32, (N_HEADS, 1), 0)
    slope_col = jnp.zeros((N_HEADS, 1), F32)
    sink_col = jnp.zeros((N_HEADS, 1), F32)
    for hd in range(N_HEADS):
        slope_col = jnp.where(hrow == hd, slopes_ref[hd], slope_col)
        sink_col = jnp.where(hrow == hd, sinks_ref[hd], sink_col)
    dist = (buf_len - lax.broadcasted_iota(jnp.int32, (N_HEADS, buf_len), 1)).astype(F32)
    bias = -slope_col * dist
    head_seg = (lax.broadcasted_iota(jnp.int32, (N_HEADS, KV_DIM), 0) // GROUP
                == lax.broadcasted_iota(jnp.int32, (N_HEADS, KV_DIM), 1) // HEAD_DIM)

    def one_sample(j, carry):
        b = i * SAMPLE_BLOCK + j
        qx = qx_sc[b]
        kn = kn_ref[pl.ds(b, 1), :]
        vn = vn_ref[pl.ds(b, 1), :]
        s = _dot_nt(qx.astype(BF16), ks_ref[j].astype(BF16)) + bias
        s_new = jnp.sum(qx.astype(BF16).astype(F32) * kn.astype(BF16).astype(F32), axis=-1, keepdims=True)
        m = jnp.maximum(jnp.maximum(jnp.max(s, axis=-1, keepdims=True), s_new), sink_col)
        p = jnp.exp(s - m)
        p_new = jnp.exp(s_new - m)
        denom = jnp.sum(p, axis=-1, keepdims=True) + p_new + jnp.exp(sink_col - m)
        pv = _dot(p.astype(BF16), vs_ref[j].astype(BF16))
        pv = pv + p_new.astype(BF16).astype(F32) * vn.astype(BF16).astype(F32)
        ox_sc[b] = jnp.where(head_seg, pv / denom, 0.0)
        return carry

    lax.fori_loop(0, SAMPLE_BLOCK, one_sample, 0)

    @pl.when(i == nsteps - 1)
    def _():
        groups = []
        for gi in range(GROUP):
            og = ox_sc[:, gi, :]
            for k in range(1, N_KV_HEADS):
                og = og + ox_sc[:, k * GROUP + gi, :]
            groups.append(og.astype(BF16))
        o_ref[...] = x_ref[...] + _dot(jnp.concatenate(groups, axis=1), wo_ref[...])


def _attn_sample(x, k_state, v_state, k_new, v_new, slopes, sinks, g, wq, qg, wo):
    n, buf_len = k_state.shape[0], k_state.shape[1]
    nsteps = n // SAMPLE_BLOCK
    st_spec = pl.BlockSpec((SAMPLE_BLOCK, buf_len, KV_DIM), lambda i: (i, 0, 0))
    smem = pl.BlockSpec(memory_space=pltpu.SMEM)
    return pl.pallas_call(
        functools.partial(_attn_sample_kernel, nsteps=nsteps, buf_len=buf_len),
        grid=(nsteps,),
        in_specs=[smem, smem,
                  _const_spec((n, D_MODEL)),
                  st_spec, st_spec,
                  _const_spec((n, KV_DIM)),
                  _const_spec((n, KV_DIM)),
                  _const_spec((1, D_MODEL)),
                  _const_spec((D_MODEL, D_MODEL)),
                  _const_spec((1, KV_DIM)),
                  _const_spec((D_MODEL, D_MODEL))],
        out_specs=pl.BlockSpec((n, D_MODEL), lambda i: (0, 0)),
        out_shape=jax.ShapeDtypeStruct((n, D_MODEL), F32),
        scratch_shapes=[pltpu.VMEM((n, N_HEADS, KV_DIM), F32),
                        pltpu.VMEM((n, N_HEADS, KV_DIM), F32)],
        compiler_params=_params(),
        name="attn_sample",
    )(slopes, sinks, x, k_state, v_state, k_new, v_new, g, wq, qg, wo)


def _window_shift_kernel(ks_ref, vs_ref, kn_ref, vn_ref, ko_ref, vo_ref, *, buf_len):
    i = pl.program_id(0)

    def one_sample(j, carry):
        b = i * SAMPLE_BLOCK + j
        last = lax.broadcasted_iota(jnp.int32, (buf_len, KV_DIM), 0) == buf_len - 1
        for s_ref, n_ref, out_ref in ((ks_ref, kn_ref, ko_ref), (vs_ref, vn_ref, vo_ref)):
            shifted = pltpu.roll(s_ref[j], buf_len - 1, 0)
            out_ref[j] = jnp.where(last, n_ref[pl.ds(b, 1), :], shifted)
        return carry

    lax.fori_loop(0, SAMPLE_BLOCK, one_sample, 0)


def _window_shift(k_state, v_state, k_new, v_new):
    n, buf_len = k_state.shape[0], k_state.shape[1]
    st_spec = pl.BlockSpec((SAMPLE_BLOCK, buf_len, KV_DIM), lambda i: (i, 0, 0))
    new_spec = _const_spec((n, KV_DIM))
    return pl.pallas_call(
        functools.partial(_window_shift_kernel, buf_len=buf_len),
        grid=(n // SAMPLE_BLOCK,),
        in_specs=[st_spec, st_spec, new_spec, new_spec],
        out_specs=[st_spec, st_spec],
        out_shape=[jax.ShapeDtypeStruct(k_state.shape, F32)] * 2,
        compiler_params=_params(),
        name="window_shift",
    )(k_state, v_state, k_new, v_new)


def _regroup_cols(w):
    return w.reshape(D_MODEL, N_KV_HEADS, GROUP, HEAD_DIM).transpose(0, 2, 1, 3).reshape(D_MODEL, D_MODEL)


def _regroup_rows(w):
    return w.reshape(N_KV_HEADS, GROUP, HEAD_DIM, D_MODEL).transpose(1, 0, 2, 3).reshape(D_MODEL, D_MODEL)


def kernel(x_prompt, x_sample, state_conv, state_win_k, state_win_v, norm_mix_g, norm_mlp_g, conv_w1, conv_b1, conv_dw, conv_dwb, conv_ln_g, conv_ln_b, conv_w2, conv_b2, kv_norm_g, w_kv, k_norm_g, attn_wq, q_norm_g, attn_sinks, attn_wo, mlp_w1, mlp_w2):
    bp, sp, _ = x_prompt.shape
    bs = x_sample.shape[0]
    n_a = conv_w1.shape[0]
    depth = mlp_w1.shape[0]
    buf_len = state_win_k.shape[1]
    assert x_sample.shape[1] == 1 and sp % TM == 0 and buf_len == WINDOW and bs % SAMPLE_BLOCK == 0

    xp = x_prompt.reshape(bp * sp, D_MODEL)
    xs = x_sample.reshape(bs, D_MODEL)
    row = lambda v: v.reshape(1, -1)
    slopes = jnp.exp2(-8.0 * jnp.arange(1, N_HEADS + 1, dtype=F32) / N_HEADS)
    head_gain = lambda gvec: jnp.tile(gvec, N_KV_HEADS).reshape(1, KV_DIM)

    conv_p, conv_s = [], []
    for l in range(depth):
        if l < n_a:
            cw = (row(norm_mix_g[l]), conv_w1[l].astype(BF16), row(conv_b1[l]), conv_dw[l], row(conv_dwb[l]),
                  row(conv_ln_g[l]), row(conv_ln_b[l]), conv_w2[l].astype(BF16), row(conv_b2[l]))
            xp, carry = _amix_prompt(xp, bp, *cw)
            xs, st = _amix_sample(xs, state_conv[l], *cw)
            conv_p.append(carry[:, CARRY - CONV_PREV:, :])
            conv_s.append(st)
        else:
            if l == n_a:
                kvw = (row(kv_norm_g), w_kv.astype(BF16), head_gain(k_norm_g))
                kb, vb, kf, vf = _kv_proj(xp, bp, *kvw, tm=TM, keep=min(WINDOW, sp))
                _, _, kn, vn = _kv_proj(xs, 1, *kvw, tm=bs, keep=bs)
                new_ks, new_vs = _window_shift(state_win_k.reshape(bs, buf_len, KV_DIM),
                                               state_win_v.reshape(bs, buf_len, KV_DIM), kn, vn)
            bl = l - n_a
            aw = (slopes, attn_sinks[bl], row(norm_mix_g[l]), _regroup_cols(attn_wq[bl]).astype(BF16),
                  head_gain(q_norm_g[bl]), _regroup_rows(attn_wo[bl]).astype(BF16))
            xp = _attn_prompt(xp, bp, kb, vb, *aw)
            xs = _attn_sample(xs, state_win_k.reshape(bs, buf_len, KV_DIM),
                              state_win_v.reshape(bs, buf_len, KV_DIM), kn, vn, *aw)
        mw = (row(norm_mlp_g[l]), mlp_w1[l].astype(BF16), mlp_w2[l].astype(BF16))
        xp = _mlp(xp, *mw, tm=TM)
        xs = _mlp(xs, *mw, tm=bs)

    keep = min(WINDOW, sp)
    kv_shape = (N_KV_HEADS, HEAD_DIM)
    return (xp.reshape(bp, sp, D_MODEL), xs.reshape(bs, 1, D_MODEL),
            jnp.stack(conv_p, axis=0), jnp.stack(conv_s, axis=0),
            kf.reshape(bp, keep, *kv_shape), vf.reshape(bp, keep, *kv_shape),
            new_ks.reshape(bs, buf_len, *kv_shape), new_vs.reshape(bs, buf_len, *kv_shape))
```

```python
import functools

import jax
import jax.numpy as jnp
from jax import lax
from jax.experimental import pallas as pl
from jax.experimental.pallas import tpu as pltpu

F32 = jnp.float32
BF16 = jnp.bfloat16

D_MODEL = 1024
N_HEADS = 16
HEAD_DIM = 64
N_KV_HEADS = 4
GROUP = 4
KV_DIM = N_KV_HEADS * HEAD_DIM
WINDOW = 128
BLOCK = 128
CONV_WIDTH = 31
CONV_PREV = CONV_WIDTH - 1
CARRY = 32
D_FF = 4 * D_MODEL
RMS_EPS = 1e-6
LN_EPS = 1e-5

SUBLANES = 8
LANES = 128
TM = 512
FF_CHUNK = 1024
CONV_ROWS = 64
STATE_COLS = 256
SAMPLE_BLOCK = 32
VMEM_LIMIT = 56 * 1024 * 1024


def _const_spec(shape):
    nd = len(shape)
    return pl.BlockSpec(shape, lambda *_: (0,) * nd, pipeline_mode=pl.Buffered(1))


def _layer_spec(shape, layer):
    nd = len(shape)
    return pl.BlockSpec((None,) + tuple(shape), lambda *_: (layer,) + (0,) * nd, pipeline_mode=pl.Buffered(1))


def _params(n_axes=1):
    return pltpu.CompilerParams(dimension_semantics=("arbitrary",) * n_axes, vmem_limit_bytes=VMEM_LIMIT)


def _rms(x, g):
    return x * lax.rsqrt(jnp.mean(x * x, axis=-1, keepdims=True) + RMS_EPS) * g


def _dot(a, b):
    return jnp.dot(a, b, preferred_element_type=F32)


def _dot_nt(a, b):
    return lax.dot_general(a, b, (((1,), (1,)), ((), ())), preferred_element_type=F32)


def _segment_ones():
    r = lax.broadcasted_iota(jnp.int32, (KV_DIM, KV_DIM), 0) // HEAD_DIM
    c = lax.broadcasted_iota(jnp.int32, (KV_DIM, KV_DIM), 1) // HEAD_DIM
    return (r == c).astype(BF16)


def _head_rms(z, g, seg):
    s = z * z
    hi = s.astype(BF16)
    lo = (s - hi.astype(F32)).astype(BF16)
    ms = (_dot(hi, seg) + _dot(lo, seg)) * (1.0 / HEAD_DIM)
    return z * lax.rsqrt(ms + RMS_EPS) * g


def _mlp_kernel(x_ref, g_ref, w1_ref, w2_ref, o_ref):
    x = x_ref[...]
    h = _rms(x, g_ref[...]).astype(BF16)
    acc = x
    for c in range(D_FF // FF_CHUNK):
        a = _dot(h, w1_ref[:, c * FF_CHUNK:(c + 1) * FF_CHUNK])
        a = jnp.maximum(a, 0.0)
        a = (a * a).astype(BF16)
        acc = acc + _dot(a, w2_ref[c * FF_CHUNK:(c + 1) * FF_CHUNK, :])
    o_ref[...] = acc


def _mlp(x, layer, g, w1, w2, tm):
    rows = x.shape[0]
    return pl.pallas_call(
        _mlp_kernel,
        grid=(rows // tm,),
        in_specs=[pl.BlockSpec((tm, D_MODEL), lambda i: (i, 0)),
                  _layer_spec((1, D_MODEL), layer),
                  _layer_spec((D_MODEL, D_FF), layer),
                  _layer_spec((D_FF, D_MODEL), layer)],
        out_specs=pl.BlockSpec((tm, D_MODEL), lambda i: (i, 0)),
        out_shape=jax.ShapeDtypeStruct((rows, D_MODEL), F32),
        compiler_params=_params(),
        name="mlp",
    )(x, g, w1, w2)


def _glu_in(x, g, w1_ref, b1_ref):
    h = _rms(x, g).astype(BF16)
    a = _dot(h, w1_ref[...]) + b1_ref[...]
    return a[:, :D_MODEL] * jax.nn.sigmoid(a[:, D_MODEL:])


def _conv_out(x, c, lng, lnb, w2_ref, b2_ref):
    mu = jnp.mean(c, axis=-1, keepdims=True)
    d = c - mu
    var = jnp.mean(d * d, axis=-1, keepdims=True)
    y = d * lax.rsqrt(var + LN_EPS) * lng + lnb
    y = y * jax.nn.sigmoid(y)
    return x + _dot(y.astype(BF16), w2_ref[...]) + b2_ref[...]


def _amix_specs(layer):
    return [_layer_spec((1, D_MODEL), layer),
            _layer_spec((D_MODEL, 2 * D_MODEL), layer),
            _layer_spec((1, 2 * D_MODEL), layer),
            _layer_spec((CONV_WIDTH, D_MODEL), layer),
            _layer_spec((1, D_MODEL), layer),
            _layer_spec((1, D_MODEL), layer),
            _layer_spec((1, D_MODEL), layer),
            _layer_spec((D_MODEL, D_MODEL), layer),
            _layer_spec((1, D_MODEL), layer)]


def _amix_prompt_kernel(x_ref, g_ref, w1_ref, b1_ref, dw_ref, dwb_ref, lng_ref, lnb_ref, w2_ref, b2_ref,
                        o_ref, carry_ref, ubuf, cbuf, *, nt):
    t = pl.program_id(0) % nt

    @pl.when(t == 0)
    def _():
        ubuf[0:CARRY, :] = jnp.zeros((CARRY, D_MODEL), F32)

    x = x_ref[...]
    ubuf[CARRY:CARRY + TM, :] = _glu_in(x, g_ref[...], w1_ref, b1_ref)

    first = CARRY - CONV_PREV

    def conv_cols(ci, carry):
        cols = pl.ds(pl.multiple_of(ci * LANES, LANES), LANES)
        for r0 in range(0, TM, CONV_ROWS):
            acc = jnp.broadcast_to(dwb_ref[:, cols], (CONV_ROWS, LANES))
            for b in range(SUBLANES):
                rows = CONV_ROWS if b == 0 else CONV_ROWS + SUBLANES
                part = None
                for o in range(b, CARRY + 1, SUBLANES):
                    if o < first:
                        continue
                    term = ubuf[r0 + o - b:r0 + o - b + rows, cols] * dw_ref[o - first:o - first + 1, cols]
                    part = term if part is None else part + term
                acc = acc + part[b:b + CONV_ROWS]
            cbuf[r0:r0 + CONV_ROWS, cols] = acc
        return carry

    lax.fori_loop(0, D_MODEL // LANES, conv_cols, 0)

    tail = ubuf[TM:TM + CARRY, :]
    carry_ref[...] = tail
    ubuf[0:CARRY, :] = tail

    o_ref[...] = _conv_out(x, cbuf[...], lng_ref[...], lnb_ref[...], w2_ref, b2_ref)


def _amix_prompt(x, batch, layer, weights):
    rows = x.shape[0]
    nt = rows // batch // TM
    return pl.pallas_call(
        functools.partial(_amix_prompt_kernel, nt=nt),
        grid=(rows // TM,),
        in_specs=[pl.BlockSpec((TM, D_MODEL), lambda i: (i, 0))] + _amix_specs(layer),
        out_specs=[pl.BlockSpec((TM, D_MODEL), lambda i: (i, 0)),
                   pl.BlockSpec((None, CARRY, D_MODEL), lambda i: (i // nt, 0, 0))],
        out_shape=[jax.ShapeDtypeStruct((rows, D_MODEL), F32),
                   jax.ShapeDtypeStruct((batch, CARRY, D_MODEL), F32)],
        scratch_shapes=[pltpu.VMEM((CARRY + TM, D_MODEL), F32),
                        pltpu.VMEM((TM, D_MODEL), F32)],
        compiler_params=_params(),
        name="amix_prompt",
    )(x, *weights)


def _amix_sample_kernel(x_ref, st_ref, g_ref, w1_ref, b1_ref, dw_ref, dwb_ref, lng_ref, lnb_ref, w2_ref, b2_ref,
                        o_ref, u_ref, c_sc, *, nsteps):
    i = pl.program_id(0)

    @pl.when(i == 0)
    def _():
        u_ref[...] = _glu_in(x_ref[...], g_ref[...], w1_ref, b1_ref)

    cols = pl.ds(pl.multiple_of(i * STATE_COLS, STATE_COLS), STATE_COLS)
    acc = dwb_ref[:, cols] + u_ref[:, cols] * dw_ref[CONV_PREV:CONV_WIDTH, cols]
    for j in range(CONV_PREV):
        acc = acc + st_ref[j] * dw_ref[j:j + 1, cols]
    c_sc[:, cols] = acc

    @pl.when(i == nsteps - 1)
    def _():
        o_ref[...] = _conv_out(x_ref[...], c_sc[...], lng_ref[...], lnb_ref[...], w2_ref, b2_ref)


def _amix_sample(x, state, layer, weights):
    n = x.shape[0]
    nsteps = D_MODEL // STATE_COLS
    return pl.pallas_call(
        functools.partial(_amix_sample_kernel, nsteps=nsteps),
        grid=(nsteps,),
        in_specs=[_const_spec((n, D_MODEL)),
                  pl.BlockSpec((None, CONV_PREV, n, STATE_COLS), lambda i: (layer, 0, 0, i))] + _amix_specs(layer),
        out_specs=[pl.BlockSpec((n, D_MODEL), lambda i: (0, 0)),
                   pl.BlockSpec((n, D_MODEL), lambda i: (0, 0))],
        out_shape=[jax.ShapeDtypeStruct((n, D_MODEL), F32),
                   jax.ShapeDtypeStruct((n, D_MODEL), F32)],
        scratch_shapes=[pltpu.VMEM((n, D_MODEL), F32)],
        compiler_params=_params(),
        name="amix_sample",
    )(x, state, *weights)


def _conv_state_kernel(st_ref, u_ref, o_ref):
    for j in range(1, CONV_PREV):
        o_ref[j - 1] = st_ref[j]
    o_ref[CONV_PREV - 1] = u_ref[...]


def _conv_state(state, u):
    n_layers, _, n, _ = state.shape
    st_spec = pl.BlockSpec((None, CONV_PREV, n, STATE_COLS), lambda a, c: (a, 0, 0, c))
    return pl.pallas_call(
        _conv_state_kernel,
        grid=(n_layers, D_MODEL // STATE_COLS),
        in_specs=[st_spec, pl.BlockSpec((None, n, STATE_COLS), lambda a, c: (a, 0, c))],
        out_specs=st_spec,
        out_shape=jax.ShapeDtypeStruct(state.shape, F32),
        compiler_params=_params(2),
        name="conv_state",
    )(state, u)


def _kv_kernel(x_ref, g_ref, w_ref, kg_ref, kb_ref, vb_ref, kf_ref, vf_ref, *, keep):
    h = _rms(x_ref[...], g_ref[...]).astype(BF16)
    kv = _dot(h, w_ref[...])
    k = _head_rms(kv[:, :KV_DIM], kg_ref[...], _segment_ones())
    v = kv[:, KV_DIM:]
    kb_ref[...] = k.astype(BF16)
    vb_ref[...] = v.astype(BF16)
    tm = k.shape[0]
    kf_ref[...] = k[tm - keep:, :]
    vf_ref[...] = v[tm - keep:, :]


def _kv_proj(x, groups, g, w, kg, tm, keep):
    rows = x.shape[0]
    nt = rows // groups // tm
    row_spec = pl.BlockSpec((tm, KV_DIM), lambda i: (i, 0))
    keep_spec = pl.BlockSpec((keep, KV_DIM), lambda i: (i // nt, 0))
    return pl.pallas_call(
        functools.partial(_kv_kernel, keep=keep),
        grid=(rows // tm,),
        in_specs=[pl.BlockSpec((tm, D_MODEL), lambda i: (i, 0)),
                  _const_spec((1, D_MODEL)),
                  _const_spec((D_MODEL, 2 * KV_DIM)),
                  _const_spec((1, KV_DIM))],
        out_specs=[row_spec, row_spec, keep_spec, keep_spec],
        out_shape=[jax.ShapeDtypeStruct((rows, KV_DIM), BF16),
                   jax.ShapeDtypeStruct((rows, KV_DIM), BF16),
                   jax.ShapeDtypeStruct((groups * keep, KV_DIM), F32),
                   jax.ShapeDtypeStruct((groups * keep, KV_DIM), F32)],
        compiler_params=_params(),
        name="kv_proj",
    )(x, g, w, kg)


def _queries(x, g, wq_ref, qg, seg):
    h = _rms(x, g).astype(BF16)
    q = _dot(h, wq_ref[...])
    out = []
    for gi in range(GROUP):
        qn = _head_rms(q[:, gi * KV_DIM:(gi + 1) * KV_DIM], qg, seg)
        out.append(qn * (HEAD_DIM ** -0.5))
    return out


def _lane_segment(k, rows):
    lane = lax.broadcasted_iota(jnp.int32, (rows, KV_DIM), 1)
    return (lane >= k * HEAD_DIM) & (lane < (k + 1) * HEAD_DIM)


def _attn_specs(layer):
    return [_layer_spec((1, D_MODEL), layer),
            _layer_spec((D_MODEL, D_MODEL), layer),
            _layer_spec((1, KV_DIM), layer),
            _layer_spec((D_MODEL, D_MODEL), layer)]


def _attn_prompt_kernel(slopes_ref, sinks_ref, x_ref, kp_ref, kc_ref, vp_ref, vc_ref, g_ref, wq_ref, qg_ref, wo_ref,
                        o_ref, bias_sc, attn_sc, *, nt, layer):
    i = pl.program_id(0)
    t = i % nt
    delta = (BLOCK + lax.broadcasted_iota(jnp.int32, (BLOCK, 2 * BLOCK), 0)
             - lax.broadcasted_iota(jnp.int32, (BLOCK, 2 * BLOCK), 1))
    band = (delta >= 0) & (delta <= WINDOW)

    @pl.when(i == 0)
    def _():
        df = delta.astype(F32)
        for hd in range(N_HEADS):
            bias_sc[hd] = jnp.where(band, -slopes_ref[hd] * df, -jnp.inf)

    x = x_ref[...]
    seg = _segment_ones()
    qs = _queries(x, g_ref[...], wq_ref, qg_ref[...], seg)
    kcur = kc_ref[...]
    vcur = vc_ref[...]
    col = lax.broadcasted_iota(jnp.int32, (GROUP * BLOCK, 2 * BLOCK), 1)
    no_prev = jnp.where((col < BLOCK) & (t == 0), -jnp.inf, 0.0)

    for qb in range(TM // BLOCK):
        rows = slice(qb * BLOCK, (qb + 1) * BLOCK)
        if qb == 0:
            kctx = jnp.concatenate([kp_ref[...], kcur[:BLOCK]], axis=0)
            vctx = jnp.concatenate([vp_ref[...], vcur[:BLOCK]], axis=0)
        else:
            kctx = kcur[(qb - 1) * BLOCK:(qb + 1) * BLOCK]
            vctx = vcur[(qb - 1) * BLOCK:(qb + 1) * BLOCK]
        for k in range(N_KV_HEADS):
            seg_k = _lane_segment(k, BLOCK)
            qm = jnp.concatenate([jnp.where(seg_k, qs[gi][rows], 0.0) for gi in range(GROUP)], axis=0)
            s = _dot_nt(qm.astype(BF16), kctx)
            s = s + bias_sc[k * GROUP:(k + 1) * GROUP].reshape(GROUP * BLOCK, 2 * BLOCK)
            if qb == 0:
                s = s + no_prev
            ps, inv = [], []
            for gi in range(GROUP):
                sg = s[gi * BLOCK:(gi + 1) * BLOCK]
                sink = sinks_ref[layer, k * GROUP + gi]
                m = jnp.maximum(jnp.max(sg, axis=-1, keepdims=True), sink)
                p = jnp.exp(sg - m)
                inv.append(1.0 / (jnp.sum(p, axis=-1, keepdims=True) + jnp.exp(sink - m)))
                ps.append(p.astype(BF16))
            pv = _dot(jnp.concatenate(ps, axis=0), vctx)
            for gi in range(GROUP):
                og = pv[gi * BLOCK:(gi + 1) * BLOCK] * inv[gi]
                lanes = slice(gi * KV_DIM + k * HEAD_DIM, gi * KV_DIM + (k + 1) * HEAD_DIM)
                attn_sc[rows, lanes] = og[:, k * HEAD_DIM:(k + 1) * HEAD_DIM]

    o_ref[...] = x + _dot(attn_sc[...].astype(BF16), wo_ref[...])


def _attn_prompt(x, batch, kb, vb, slopes, sinks, layer, weights):
    rows = x.shape[0]
    nt = rows // batch // TM
    per = TM // BLOCK
    prev_spec = pl.BlockSpec((BLOCK, KV_DIM), lambda i: (jnp.maximum(i * per - 1, 0), 0))
    cur_spec = pl.BlockSpec((TM, KV_DIM), lambda i: (i, 0))
    smem = pl.BlockSpec(memory_space=pltpu.SMEM)
    return pl.pallas_call(
        functools.partial(_attn_prompt_kernel, nt=nt, layer=layer),
        grid=(rows // TM,),
        in_specs=[smem, smem,
                  pl.BlockSpec((TM, D_MODEL), lambda i: (i, 0)),
                  prev_spec, cur_spec, prev_spec, cur_spec] + _attn_specs(layer),
        out_specs=pl.BlockSpec((TM, D_MODEL), lambda i: (i, 0)),
        out_shape=jax.ShapeDtypeStruct((rows, D_MODEL), F32),
        scratch_shapes=[pltpu.VMEM((N_HEADS, BLOCK, 2 * BLOCK), F32),
                        pltpu.VMEM((TM, D_MODEL), F32)],
        compiler_params=_params(),
        name="attn_prompt",
    )(slopes, sinks, x, kb, kb, vb, vb, *weights)


def _attn_sample_kernel(slopes_ref, sinks_ref, x_ref, ks_ref, vs_ref, kn_ref, vn_ref, g_ref, wq_ref, qg_ref, wo_ref,
                        o_ref, qx_sc, ox_sc, *, nsteps, buf_len, layer):
    i = pl.program_id(0)
    n = x_ref.shape[0]

    @pl.when(i == 0)
    def _():
        qs = _queries(x_ref[...], g_ref[...], wq_ref, qg_ref[...], _segment_ones())
        for k in range(N_KV_HEADS):
            seg_k = _lane_segment(k, n)
            for gi in range(GROUP):
                qx_sc[:, k * GROUP + gi, :] = jnp.where(seg_k, qs[gi], 0.0)

    hrow = lax.broadcasted_iota(jnp.int32, (N_HEADS, 1), 0)
    slope_col = jnp.zeros((N_HEADS, 1), F32)
    sink_col = jnp.zeros((N_HEADS, 1), F32)
    for hd in range(N_HEADS):
        slope_col = jnp.where(hrow == hd, slopes_ref[hd], slope_col)
        sink_col = jnp.where(hrow == hd, sinks_ref[layer, hd], sink_col)
    dist = (buf_len - lax.broadcasted_iota(jnp.int32, (N_HEADS, buf_len), 1)).astype(F32)
    bias = -slope_col * dist
    head_seg = (lax.broadcasted_iota(jnp.int32, (N_HEADS, KV_DIM), 0) // GROUP
                == lax.broadcasted_iota(jnp.int32, (N_HEADS, KV_DIM), 1) // HEAD_DIM)

    def one_sample(j, carry):
        b = i * SAMPLE_BLOCK + j
        qx = qx_sc[b].astype(BF16)
        kn = kn_ref[pl.ds(b, 1), :].astype(BF16).astype(F32)
        vn = vn_ref[pl.ds(b, 1), :].astype(BF16).astype(F32)
        s = _dot(qx, ks_ref[j].astype(BF16)) + bias
        s_new = jnp.sum(qx.astype(F32) * kn, axis=-1, keepdims=True)
        m = jnp.maximum(jnp.maximum(jnp.max(s, axis=-1, keepdims=True), s_new), sink_col)
        p = jnp.exp(s - m)
        p_new = jnp.exp(s_new - m)
        denom = jnp.sum(p, axis=-1, keepdims=True) + p_new + jnp.exp(sink_col - m)
        pv = _dot_nt(p.astype(BF16), vs_ref[j].astype(BF16)) + p_new.astype(BF16).astype(F32) * vn
        ox_sc[b] = jnp.where(head_seg, pv / denom, 0.0)
        return carry

    lax.fori_loop(0, SAMPLE_BLOCK, one_sample, 0, unroll=4)

    @pl.when(i == nsteps - 1)
    def _():
        groups = []
        for gi in range(GROUP):
            og = ox_sc[:, gi, :]
            for k in range(1, N_KV_HEADS):
                og = og + ox_sc[:, k * GROUP + gi, :]
            groups.append(og.astype(BF16))
        o_ref[...] = x_ref[...] + _dot(jnp.concatenate(groups, axis=1), wo_ref[...])


def _attn_sample(x, k_state, v_state, k_new, v_new, slopes, sinks, layer, weights):
    n, buf_len = k_state.shape[0], k_state.shape[2]
    nsteps = n // SAMPLE_BLOCK
    st_spec = pl.BlockSpec((SAMPLE_BLOCK, KV_DIM, buf_len), lambda i: (i, 0, 0))
    smem = pl.BlockSpec(memory_space=pltpu.SMEM)
    return pl.pallas_call(
        functools.partial(_attn_sample_kernel, nsteps=nsteps, buf_len=buf_len, layer=layer),
        grid=(nsteps,),
        in_specs=[smem, smem,
                  _const_spec((n, D_MODEL)),
                  st_spec, st_spec,
                  _const_spec((n, KV_DIM)),
                  _const_spec((n, KV_DIM))] + _attn_specs(layer),
        out_specs=pl.BlockSpec((n, D_MODEL), lambda i: (0, 0)),
        out_shape=jax.ShapeDtypeStruct((n, D_MODEL), F32),
        scratch_shapes=[pltpu.VMEM((n, N_HEADS, KV_DIM), F32),
                        pltpu.VMEM((n, N_HEADS, KV_DIM), F32)],
        compiler_params=_params(),
        name="attn_sample",
    )(slopes, sinks, x, k_state, v_state, k_new, v_new, *weights)


def _window_shift_kernel(ks_ref, vs_ref, kn_ref, vn_ref, ko_ref, vo_ref, knt_sc, vnt_sc, *, buf_len):
    i = pl.program_id(0)

    @pl.when(i == 0)
    def _():
        knt_sc[...] = kn_ref[...].T
        vnt_sc[...] = vn_ref[...].T

    newest = lax.broadcasted_iota(jnp.int32, (KV_DIM, buf_len), 1) == buf_len - 1

    def one_sample(j, carry):
        b = i * SAMPLE_BLOCK + j
        for s_ref, nt_sc, out_ref in ((ks_ref, knt_sc, ko_ref), (vs_ref, vnt_sc, vo_ref)):
            shifted = pltpu.roll(s_ref[j], buf_len - 1, 1)
            new_col = pltpu.roll(nt_sc[...], buf_len - 1 - b, 1)
            out_ref[j] = jnp.where(newest, new_col, shifted)
        return carry

    lax.fori_loop(0, SAMPLE_BLOCK, one_sample, 0)


def _window_shift(k_state, v_state, k_new, v_new):
    n, buf_len = k_state.shape[0], k_state.shape[2]
    assert n == buf_len
    st_spec = pl.BlockSpec((SAMPLE_BLOCK, KV_DIM, buf_len), lambda i: (i, 0, 0))
    new_spec = _const_spec((n, KV_DIM))
    return pl.pallas_call(
        functools.partial(_window_shift_kernel, buf_len=buf_len),
        grid=(n // SAMPLE_BLOCK,),
        in_specs=[st_spec, st_spec, new_spec, new_spec],
        out_specs=[st_spec, st_spec],
        out_shape=[jax.ShapeDtypeStruct(k_state.shape, F32)] * 2,
        scratch_shapes=[pltpu.VMEM((KV_DIM, n), F32)] * 2,
        compiler_params=_params(),
        name="window_shift",
    )(k_state, v_state, k_new, v_new)


def _regroup_cols(w):
    n = w.shape[0]
    w = w.reshape(n, D_MODEL, N_KV_HEADS, GROUP, HEAD_DIM).transpose(0, 1, 3, 2, 4)
    return w.reshape(n, D_MODEL, D_MODEL)


def _regroup_rows(w):
    n = w.shape[0]
    w = w.reshape(n, N_KV_HEADS, GROUP, HEAD_DIM, D_MODEL).transpose(0, 2, 1, 3, 4)
    return w.reshape(n, D_MODEL, D_MODEL)


def kernel(x_prompt, x_sample, state_conv, state_win_k, state_win_v, norm_mix_g, norm_mlp_g, conv_w1, conv_b1, conv_dw, conv_dwb, conv_ln_g, conv_ln_b, conv_w2, conv_b2, kv_norm_g, w_kv, k_norm_g, attn_wq, q_norm_g, attn_sinks, attn_wo, mlp_w1, mlp_w2):
    bp, sp, _ = x_prompt.shape
    bs = x_sample.shape[0]
    n_a = conv_w1.shape[0]
    depth = mlp_w1.shape[0]
    buf_len = state_win_k.shape[1]
    assert x_sample.shape[1] == 1 and sp % TM == 0 and buf_len == WINDOW and bs % SAMPLE_BLOCK == 0

    xp = x_prompt.reshape(bp * sp, D_MODEL)
    xs = x_sample.reshape(bs, D_MODEL)
    rows3 = lambda v: v.reshape(v.shape[0], 1, v.shape[-1])
    slopes = jnp.exp2(-8.0 * jnp.arange(1, N_HEADS + 1, dtype=F32) / N_HEADS)
    head_gain = lambda gvec: jnp.tile(gvec, (1, N_KV_HEADS)).reshape(gvec.shape[0], 1, KV_DIM)

    mix_g, mlp_g = rows3(norm_mix_g), rows3(norm_mlp_g)
    conv_weights = (mix_g, conv_w1.astype(BF16), rows3(conv_b1), conv_dw, rows3(conv_dwb),
                    rows3(conv_ln_g), rows3(conv_ln_b), conv_w2.astype(BF16), rows3(conv_b2))
    mlp_weights = (mlp_g, mlp_w1.astype(BF16), mlp_w2.astype(BF16))
    kv_weights = (kv_norm_g.reshape(1, D_MODEL), w_kv.astype(BF16), jnp.tile(k_norm_g, N_KV_HEADS).reshape(1, KV_DIM))
    attn_weights = (mix_g[n_a:], _regroup_cols(attn_wq).astype(BF16), head_gain(q_norm_g),
                    _regroup_rows(attn_wo).astype(BF16))

    conv_state = state_conv.transpose(0, 2, 1, 3)
    k_state = state_win_k.transpose(0, 2, 3, 1).reshape(bs, KV_DIM, buf_len)
    v_state = state_win_v.transpose(0, 2, 3, 1).reshape(bs, KV_DIM, buf_len)

    conv_p, conv_u = [], []
    for l in range(depth):
        if l < n_a:
            xp, carry = _amix_prompt(xp, bp, l, conv_weights)
            xs, u = _amix_sample(xs, conv_state, l, conv_weights)
            conv_p.append(carry[:, CARRY - CONV_PREV:, :])
            conv_u.append(u)
        else:
            if l == n_a:
                kb, vb, kf, vf = _kv_proj(xp, bp, *kv_weights, tm=TM, keep=min(WINDOW, sp))
                _, _, kn, vn = _kv_proj(xs, 1, *kv_weights, tm=bs, keep=bs)
                new_ks, new_vs = _window_shift(k_state, v_state, kn, vn)
            bl = l - n_a
            xp = _attn_prompt(xp, bp, kb, vb, slopes, attn_sinks, bl, attn_weights)
            xs = _attn_sample(xs, k_state, v_state, kn, vn, slopes, attn_sinks, bl, attn_weights)
        xp = _mlp(xp, l, *mlp_weights, tm=TM)
        xs = _mlp(xs, l, *mlp_weights, tm=bs)

    new_conv_s = _conv_state(conv_state, jnp.stack(conv_u, axis=0)).transpose(0, 2, 1, 3)
    keep = min(WINDOW, sp)
    kv_shape = (N_KV_HEADS, HEAD_DIM)
    window_out = lambda w: w.reshape(bs, N_KV_HEADS, HEAD_DIM, buf_len).transpose(0, 3, 1, 2)
    return (xp.reshape(bp, sp, D_MODEL), xs.reshape(bs, 1, D_MODEL),
            jnp.stack(conv_p, axis=0), new_conv_s,
            kf.reshape(bp, keep, *kv_shape), vf.reshape(bp, keep, *kv_shape),
            window_out(new_ks), window_out(new_vs))
```

```python
import functools

import jax
import jax.numpy as jnp
from jax import lax
from jax.experimental import pallas as pl
from jax.experimental.pallas import tpu as pltpu

F32 = jnp.float32
BF16 = jnp.bfloat16

D_MODEL = 1024
N_HEADS = 16
HEAD_DIM = 64
N_KV_HEADS = 4
GROUP = 4
KV_DIM = N_KV_HEADS * HEAD_DIM
WINDOW = 128
BLOCK = 128
CONV_WIDTH = 31
CONV_PREV = CONV_WIDTH - 1
CARRY = 32
D_FF = 4 * D_MODEL
RMS_EPS = 1e-6
LN_EPS = 1e-5

SUBLANES = 8
LANES = 128
TM = 512
FF_CHUNK = 1024
CONV_ROWS = 64
MXU_COLS = 512
STATE_COLS = 256
SAMPLE_BLOCK = 32
VMEM_LIMIT = 56 * 1024 * 1024


def _const_spec(shape):
    nd = len(shape)
    return pl.BlockSpec(shape, lambda *_: (0,) * nd, pipeline_mode=pl.Buffered(1))


def _layer_spec(shape, layer):
    nd = len(shape)
    return pl.BlockSpec((None,) + tuple(shape), lambda *_: (layer,) + (0,) * nd, pipeline_mode=pl.Buffered(1))


def _params(n_axes=1):
    return pltpu.CompilerParams(dimension_semantics=("arbitrary",) * n_axes, vmem_limit_bytes=VMEM_LIMIT)


def _rms(x, g):
    return x * lax.rsqrt(jnp.mean(x * x, axis=-1, keepdims=True) + RMS_EPS) * g


def _dot(a, b):
    return jnp.dot(a, b, preferred_element_type=F32)


def _dot_nt(a, b):
    return lax.dot_general(a, b, (((1,), (1,)), ((), ())), preferred_element_type=F32)


def _segment_ones():
    r = lax.broadcasted_iota(jnp.int32, (KV_DIM, KV_DIM), 0) // HEAD_DIM
    c = lax.broadcasted_iota(jnp.int32, (KV_DIM, KV_DIM), 1) // HEAD_DIM
    return (r == c).astype(BF16)


def _head_rms(z, g, seg):
    s = z * z
    hi = s.astype(BF16)
    lo = (s - hi.astype(F32)).astype(BF16)
    ms = (_dot(hi, seg) + _dot(lo, seg)) * (1.0 / HEAD_DIM)
    return z * lax.rsqrt(ms + RMS_EPS) * g


def _mlp_kernel(x_ref, g_ref, w1_ref, w2_ref, o_ref):
    x = x_ref[...]
    h = _rms(x, g_ref[...]).astype(BF16)
    acc = x
    for c in range(D_FF // FF_CHUNK):
        a = _dot(h, w1_ref[:, c * FF_CHUNK:(c + 1) * FF_CHUNK])
        a = jnp.maximum(a, 0.0)
        a = (a * a).astype(BF16)
        acc = acc + _dot(a, w2_ref[c * FF_CHUNK:(c + 1) * FF_CHUNK, :])
    o_ref[...] = acc


def _mlp(x, layer, g, w1, w2, tm):
    rows = x.shape[0]
    return pl.pallas_call(
        _mlp_kernel,
        grid=(rows // tm,),
        in_specs=[pl.BlockSpec((tm, D_MODEL), lambda i: (i, 0)),
                  _layer_spec((1, D_MODEL), layer),
                  _layer_spec((D_MODEL, D_FF), layer),
                  _layer_spec((D_FF, D_MODEL), layer)],
        out_specs=pl.BlockSpec((tm, D_MODEL), lambda i: (i, 0)),
        out_shape=jax.ShapeDtypeStruct((rows, D_MODEL), F32),
        compiler_params=_params(),
        name="mlp",
    )(x, g, w1, w2)


def _glu_in(x, g, w1_ref, b1_ref):
    h = _rms(x, g).astype(BF16)
    a = _dot(h, w1_ref[...]) + b1_ref[...]
    return a[:, :D_MODEL] * jax.nn.sigmoid(a[:, D_MODEL:])


def _conv_out(x, c, lng, lnb, w2_ref, b2_ref):
    mu = jnp.mean(c, axis=-1, keepdims=True)
    d = c - mu
    var = jnp.mean(d * d, axis=-1, keepdims=True)
    y = d * lax.rsqrt(var + LN_EPS) * lng + lnb
    y = y * jax.nn.sigmoid(y)
    return x + _dot(y.astype(BF16), w2_ref[...]) + b2_ref[...]


def _amix_specs(layer):
    return [_layer_spec((1, D_MODEL), layer),
            _layer_spec((D_MODEL, 2 * D_MODEL), layer),
            _layer_spec((1, 2 * D_MODEL), layer),
            _layer_spec((CONV_WIDTH, D_MODEL), layer),
            _layer_spec((1, D_MODEL), layer),
            _layer_spec((1, D_MODEL), layer),
            _layer_spec((1, D_MODEL), layer),
            _layer_spec((D_MODEL, D_MODEL), layer),
            _layer_spec((1, D_MODEL), layer)]


def _conv_rows(ubuf, cbuf, dw_ref, dwb_ref, piece):
    first = CARRY - CONV_PREV
    c, r = divmod(piece, TM // CONV_ROWS)
    cols = slice(c * LANES, (c + 1) * LANES)
    r0 = r * CONV_ROWS
    acc = jnp.broadcast_to(dwb_ref[:, cols], (CONV_ROWS, LANES))
    for b in range(SUBLANES):
        rows = CONV_ROWS if b == 0 else CONV_ROWS + SUBLANES
        part = None
        for o in range(b, CARRY + 1, SUBLANES):
            if o < first:
                continue
            term = ubuf[r0 + o - b:r0 + o - b + rows, cols] * dw_ref[o - first:o - first + 1, cols]
            part = term if part is None else part + term
        acc = acc + part[b:b + CONV_ROWS]
    cbuf[r0:r0 + CONV_ROWS, cols] = acc


def _conv_layer_kernel(*refs, nt, n_tiles, with_kv):
    (x_ref, g_ref, w1_ref, b1_ref, dw_ref, dwb_ref, lng_ref, lnb_ref, w2_ref, b2_ref,
     mg_ref, mw1_ref, mw2_ref) = refs[:13]
    refs = refs[13:]
    if with_kv:
        kvg_ref, wkv_ref, kg_ref = refs[:3]
        o_ref, carry_ref, kb_ref, vb_ref, kf_ref, vf_ref, ubuf, cbuf, xmid, hbuf, acc = refs[3:]
    else:
        o_ref, carry_ref, ubuf, cbuf, xmid, hbuf, acc = refs
    i = pl.program_id(0)
    mixing = i < n_tiles

    @pl.when(i == 0)
    def _():
        xmid[...] = jnp.zeros((TM, D_MODEL), F32)
        hbuf[...] = jnp.zeros((TM, D_MODEL), BF16)

    @pl.when(i % nt == 0)
    def _():
        ubuf[0:CARRY, :] = jnp.zeros((CARRY, D_MODEL), F32)

    @pl.when(mixing)
    def _():
        ubuf[CARRY:CARRY + TM, :] = _glu_in(x_ref[...], g_ref[...], w1_ref, b1_ref)

    h = hbuf[...]
    acc[...] = xmid[...]
    n_chunks = D_FF // FF_CHUNK
    n_blocks = FF_CHUNK // MXU_COLS
    conv_pieces = (D_MODEL // LANES) * (TM // CONV_ROWS)
    per_block = conv_pieces // (2 * n_chunks * n_blocks)
    assert per_block * 2 * n_chunks * n_blocks == conv_pieces
    piece = 0
    for c in range(n_chunks):
        parts = []
        for n in range(n_blocks):
            col0 = c * FF_CHUNK + n * MXU_COLS
            a = jnp.maximum(_dot(h, mw1_ref[:, col0:col0 + MXU_COLS]), 0.0)
            parts.append((a * a).astype(BF16))
            for _ in range(per_block):
                _conv_rows(ubuf, cbuf, dw_ref, dwb_ref, piece)
                piece += 1
        a = jnp.concatenate(parts, axis=1)
        for n in range(D_MODEL // MXU_COLS):
            cols = slice(n * MXU_COLS, (n + 1) * MXU_COLS)
            acc[:, cols] += _dot(a, mw2_ref[c * FF_CHUNK:(c + 1) * FF_CHUNK, cols])
            for _ in range(per_block):
                _conv_rows(ubuf, cbuf, dw_ref, dwb_ref, piece)
                piece += 1
    out = acc[...]
    o_ref[...] = out

    if with_kv:
        hk = _rms(out, kvg_ref[...]).astype(BF16)
        kv = _dot(hk, wkv_ref[...])
        k = _head_rms(kv[:, :KV_DIM], kg_ref[...], _segment_ones())
        v = kv[:, KV_DIM:]
        kb_ref[...] = k.astype(BF16)
        vb_ref[...] = v.astype(BF16)
        keep = kf_ref.shape[0]
        kf_ref[...] = k[TM - keep:, :]
        vf_ref[...] = v[TM - keep:, :]

    @pl.when(mixing)
    def _():
        tail = ubuf[TM:TM + CARRY, :]
        carry_ref[...] = tail
        ubuf[0:CARRY, :] = tail
        xm = _conv_out(x_ref[...], cbuf[...], lng_ref[...], lnb_ref[...], w2_ref, b2_ref)
        xmid[...] = xm
        hbuf[...] = _rms(xm, mg_ref[...]).astype(BF16)


def _conv_layer_prompt(x, batch, layer, conv_weights, mlp_weights, kv_weights=None, keep=None):
    rows = x.shape[0]
    n_tiles = rows // TM
    nt = n_tiles // batch
    with_kv = kv_weights is not None
    mix_tile = lambda i: jnp.minimum(i, n_tiles - 1)
    mlp_tile = lambda i: jnp.maximum(i - 1, 0)
    in_specs = ([pl.BlockSpec((TM, D_MODEL), lambda i: (mix_tile(i), 0))] + _amix_specs(layer)
                + [_layer_spec((1, D_MODEL), layer),
                   _layer_spec((D_MODEL, D_FF), layer),
                   _layer_spec((D_FF, D_MODEL), layer)])
    out_specs = [pl.BlockSpec((TM, D_MODEL), lambda i: (mlp_tile(i), 0)),
                 pl.BlockSpec((None, CARRY, D_MODEL), lambda i: (mix_tile(i) // nt, 0, 0))]
    out_shape = [jax.ShapeDtypeStruct((rows, D_MODEL), F32),
                 jax.ShapeDtypeStruct((batch, CARRY, D_MODEL), F32)]
    args = (x,) + tuple(conv_weights) + tuple(mlp_weights)
    if with_kv:
        in_specs += [_const_spec((1, D_MODEL)), _const_spec((D_MODEL, 2 * KV_DIM)), _const_spec((1, KV_DIM))]
        row_spec = pl.BlockSpec((TM, KV_DIM), lambda i: (mlp_tile(i), 0))
        keep_spec = pl.BlockSpec((keep, KV_DIM), lambda i: (mlp_tile(i) // nt, 0))
        out_specs += [row_spec, row_spec, keep_spec, keep_spec]
        out_shape += [jax.ShapeDtypeStruct((rows, KV_DIM), BF16)] * 2
        out_shape += [jax.ShapeDtypeStruct((batch * keep, KV_DIM), F32)] * 2
        args += tuple(kv_weights)
    return pl.pallas_call(
        functools.partial(_conv_layer_kernel, nt=nt, n_tiles=n_tiles, with_kv=with_kv),
        grid=(n_tiles + 1,),
        in_specs=in_specs,
        out_specs=out_specs,
        out_shape=out_shape,
        scratch_shapes=[pltpu.VMEM((CARRY + TM, D_MODEL), F32),
                        pltpu.VMEM((TM, D_MODEL), F32),
                        pltpu.VMEM((TM, D_MODEL), F32),
                        pltpu.VMEM((TM, D_MODEL), BF16),
                        pltpu.VMEM((TM, D_MODEL), F32)],
        compiler_params=_params(),
        name="conv_layer_prompt",
    )(*args)


def _amix_sample_kernel(x_ref, st_ref, g_ref, w1_ref, b1_ref, dw_ref, dwb_ref, lng_ref, lnb_ref, w2_ref, b2_ref,
                        o_ref, u_ref, c_sc, *, nsteps):
    i = pl.program_id(0)

    @pl.when(i == 0)
    def _():
        u_ref[...] = _glu_in(x_ref[...], g_ref[...], w1_ref, b1_ref)

    cols = pl.ds(pl.multiple_of(i * STATE_COLS, STATE_COLS), STATE_COLS)
    acc = dwb_ref[:, cols] + u_ref[:, cols] * dw_ref[CONV_PREV:CONV_WIDTH, cols]
    for j in range(CONV_PREV):
        acc = acc + st_ref[j] * dw_ref[j:j + 1, cols]
    c_sc[:, cols] = acc

    @pl.when(i == nsteps - 1)
    def _():
        o_ref[...] = _conv_out(x_ref[...], c_sc[...], lng_ref[...], lnb_ref[...], w2_ref, b2_ref)


def _amix_sample(x, state, layer, weights):
    n = x.shape[0]
    nsteps = D_MODEL // STATE_COLS
    return pl.pallas_call(
        functools.partial(_amix_sample_kernel, nsteps=nsteps),
        grid=(nsteps,),
        in_specs=[_const_spec((n, D_MODEL)),
                  pl.BlockSpec((None, CONV_PREV, n, STATE_COLS), lambda i: (layer, 0, 0, i))] + _amix_specs(layer),
        out_specs=[pl.BlockSpec((n, D_MODEL), lambda i: (0, 0)),
                   pl.BlockSpec((n, D_MODEL), lambda i: (0, 0))],
        out_shape=[jax.ShapeDtypeStruct((n, D_MODEL), F32),
                   jax.ShapeDtypeStruct((n, D_MODEL), F32)],
        scratch_shapes=[pltpu.VMEM((n, D_MODEL), F32)],
        compiler_params=_params(),
        name="amix_sample",
    )(x, state, *weights)


def _conv_state_kernel(st_ref, u_ref, o_ref):
    for j in range(1, CONV_PREV):
        o_ref[j - 1] = st_ref[j]
    o_ref[CONV_PREV - 1] = u_ref[...]


def _conv_state(state, u):
    n_layers, _, n, _ = state.shape
    st_spec = pl.BlockSpec((None, CONV_PREV, n, STATE_COLS), lambda a, c: (a, 0, 0, c))
    return pl.pallas_call(
        _conv_state_kernel,
        grid=(n_layers, D_MODEL // STATE_COLS),
        in_specs=[st_spec, pl.BlockSpec((None, n, STATE_COLS), lambda a, c: (a, 0, c))],
        out_specs=st_spec,
        out_shape=jax.ShapeDtypeStruct(state.shape, F32),
        compiler_params=_params(2),
        name="conv_state",
    )(state, u)


def _kv_kernel(x_ref, g_ref, w_ref, kg_ref, kb_ref, vb_ref, kf_ref, vf_ref, *, keep):
    h = _rms(x_ref[...], g_ref[...]).astype(BF16)
    kv = _dot(h, w_ref[...])
    k = _head_rms(kv[:, :KV_DIM], kg_ref[...], _segment_ones())
    v = kv[:, KV_DIM:]
    kb_ref[...] = k.astype(BF16)
    vb_ref[...] = v.astype(BF16)
    tm = k.shape[0]
    kf_ref[...] = k[tm - keep:, :]
    vf_ref[...] = v[tm - keep:, :]


def _kv_proj(x, groups, g, w, kg, tm, keep):
    rows = x.shape[0]
    nt = rows // groups // tm
    row_spec = pl.BlockSpec((tm, KV_DIM), lambda i: (i, 0))
    keep_spec = pl.BlockSpec((keep, KV_DIM), lambda i: (i // nt, 0))
    return pl.pallas_call(
        functools.partial(_kv_kernel, keep=keep),
        grid=(rows // tm,),
        in_specs=[pl.BlockSpec((tm, D_MODEL), lambda i: (i, 0)),
                  _const_spec((1, D_MODEL)),
                  _const_spec((D_MODEL, 2 * KV_DIM)),
                  _const_spec((1, KV_DIM))],
        out_specs=[row_spec, row_spec, keep_spec, keep_spec],
        out_shape=[jax.ShapeDtypeStruct((rows, KV_DIM), BF16),
                   jax.ShapeDtypeStruct((rows, KV_DIM), BF16),
                   jax.ShapeDtypeStruct((groups * keep, KV_DIM), F32),
                   jax.ShapeDtypeStruct((groups * keep, KV_DIM), F32)],
        compiler_params=_params(),
        name="kv_proj",
    )(x, g, w, kg)


def _queries(x, g, wq_ref, qg, seg):
    h = _rms(x, g).astype(BF16)
    q = _dot(h, wq_ref[...])
    out = []
    for gi in range(GROUP):
        qn = _head_rms(q[:, gi * KV_DIM:(gi + 1) * KV_DIM], qg, seg)
        out.append(qn * (HEAD_DIM ** -0.5))
    return out


def _lane_segment(k, rows):
    lane = lax.broadcasted_iota(jnp.int32, (rows, KV_DIM), 1)
    return (lane >= k * HEAD_DIM) & (lane < (k + 1) * HEAD_DIM)


def _attn_specs(layer):
    return [_layer_spec((1, D_MODEL), layer),
            _layer_spec((D_MODEL, D_MODEL), layer),
            _layer_spec((1, KV_DIM), layer),
            _layer_spec((D_MODEL, D_MODEL), layer)]


def _attn_prompt_kernel(slopes_ref, sinks_ref, x_ref, kp_ref, kc_ref, vp_ref, vc_ref, g_ref, wq_ref, qg_ref, wo_ref,
                        o_ref, bias_sc, attn_sc, *, nt, layer):
    i = pl.program_id(0)
    t = i % nt
    delta = (BLOCK + lax.broadcasted_iota(jnp.int32, (BLOCK, 2 * BLOCK), 0)
             - lax.broadcasted_iota(jnp.int32, (BLOCK, 2 * BLOCK), 1))
    band = (delta >= 0) & (delta <= WINDOW)

    @pl.when(i == 0)
    def _():
        df = delta.astype(F32)
        for hd in range(N_HEADS):
            bias_sc[hd] = jnp.where(band, -slopes_ref[hd] * df, -jnp.inf)

    x = x_ref[...]
    seg = _segment_ones()
    qs = _queries(x, g_ref[...], wq_ref, qg_ref[...], seg)
    kcur = kc_ref[...]
    vcur = vc_ref[...]
    col = lax.broadcasted_iota(jnp.int32, (GROUP * BLOCK, 2 * BLOCK), 1)
    no_prev = jnp.where((col < BLOCK) & (t == 0), -jnp.inf, 0.0)

    for qb in range(TM // BLOCK):
        rows = slice(qb * BLOCK, (qb + 1) * BLOCK)
        if qb == 0:
            kctx = jnp.concatenate([kp_ref[...], kcur[:BLOCK]], axis=0)
            vctx = jnp.concatenate([vp_ref[...], vcur[:BLOCK]], axis=0)
        else:
            kctx = kcur[(qb - 1) * BLOCK:(qb + 1) * BLOCK]
            vctx = vcur[(qb - 1) * BLOCK:(qb + 1) * BLOCK]
        for k in range(N_KV_HEADS):
            seg_k = _lane_segment(k, BLOCK)
            qm = jnp.concatenate([jnp.where(seg_k, qs[gi][rows], 0.0) for gi in range(GROUP)], axis=0)
            s = _dot_nt(qm.astype(BF16), kctx)
            s = s + bias_sc[k * GROUP:(k + 1) * GROUP].reshape(GROUP * BLOCK, 2 * BLOCK)
            if qb == 0:
                s = s + no_prev
            ps, inv = [], []
            for gi in range(GROUP):
                sg = s[gi * BLOCK:(gi + 1) * BLOCK]
                sink = sinks_ref[layer, k * GROUP + gi]
                m = jnp.maximum(jnp.max(sg, axis=-1, keepdims=True), sink)
                p = jnp.exp(sg - m)
                inv.append(1.0 / (jnp.sum(p, axis=-1, keepdims=True) + jnp.exp(sink - m)))
                ps.append(p.astype(BF16))
            pv = _dot(jnp.concatenate(ps, axis=0), vctx)
            for gi in range(GROUP):
                og = pv[gi * BLOCK:(gi + 1) * BLOCK] * inv[gi]
                lanes = slice(gi * KV_DIM + k * HEAD_DIM, gi * KV_DIM + (k + 1) * HEAD_DIM)
                attn_sc[rows, lanes] = og[:, k * HEAD_DIM:(k + 1) * HEAD_DIM]

    o_ref[...] = x + _dot(attn_sc[...].astype(BF16), wo_ref[...])


def _attn_prompt(x, batch, kb, vb, slopes, sinks, layer, weights):
    rows = x.shape[0]
    nt = rows // batch // TM
    per = TM // BLOCK
    prev_spec = pl.BlockSpec((BLOCK, KV_DIM), lambda i: (jnp.maximum(i * per - 1, 0), 0))
    cur_spec = pl.BlockSpec((TM, KV_DIM), lambda i: (i, 0))
    smem = pl.BlockSpec(memory_space=pltpu.SMEM)
    return pl.pallas_call(
        functools.partial(_attn_prompt_kernel, nt=nt, layer=layer),
        grid=(rows // TM,),
        in_specs=[smem, smem,
                  pl.BlockSpec((TM, D_MODEL), lambda i: (i, 0)),
                  prev_spec, cur_spec, prev_spec, cur_spec] + _attn_specs(layer),
        out_specs=pl.BlockSpec((TM, D_MODEL), lambda i: (i, 0)),
        out_shape=jax.ShapeDtypeStruct((rows, D_MODEL), F32),
        scratch_shapes=[pltpu.VMEM((N_HEADS, BLOCK, 2 * BLOCK), F32),
                        pltpu.VMEM((TM, D_MODEL), F32)],
        compiler_params=_params(),
        name="attn_prompt",
    )(slopes, sinks, x, kb, kb, vb, vb, *weights)


def _attn_sample_kernel(slopes_ref, sinks_ref, x_ref, ks_ref, vs_ref, kn_ref, vn_ref, g_ref, wq_ref, qg_ref, wo_ref,
                        o_ref, qx_sc, ox_sc, *, nsteps, buf_len, layer):
    i = pl.program_id(0)
    n = x_ref.shape[0]

    @pl.when(i == 0)
    def _():
        qs = _queries(x_ref[...], g_ref[...], wq_ref, qg_ref[...], _segment_ones())
        for k in range(N_KV_HEADS):
            seg_k = _lane_segment(k, n)
            for gi in range(GROUP):
                qx_sc[:, k * GROUP + gi, :] = jnp.where(seg_k, qs[gi], 0.0)

    hrow = lax.broadcasted_iota(jnp.int32, (N_HEADS, 1), 0)
    slope_col = jnp.zeros((N_HEADS, 1), F32)
    sink_col = jnp.zeros((N_HEADS, 1), F32)
    for hd in range(N_HEADS):
        slope_col = jnp.where(hrow == hd, slopes_ref[hd], slope_col)
        sink_col = jnp.where(hrow == hd, sinks_ref[layer, hd], sink_col)
    dist = (buf_len - lax.broadcasted_iota(jnp.int32, (N_HEADS, buf_len), 1)).astype(F32)
    bias = -slope_col * dist
    head_seg = (lax.broadcasted_iota(jnp.int32, (N_HEADS, KV_DIM), 0) // GROUP
                == lax.broadcasted_iota(jnp.int32, (N_HEADS, KV_DIM), 1) // HEAD_DIM)

    def one_sample(j, carry):
        b = i * SAMPLE_BLOCK + j
        qx = qx_sc[b].astype(BF16)
        kn = kn_ref[pl.ds(b, 1), :].astype(BF16).astype(F32)
        vn = vn_ref[pl.ds(b, 1), :].astype(BF16).astype(F32)
        s = _dot(qx, ks_ref[j].astype(BF16)) + bias
        s_new = jnp.sum(qx.astype(F32) * kn, axis=-1, keepdims=True)
        m = jnp.maximum(jnp.maximum(jnp.max(s, axis=-1, keepdims=True), s_new), sink_col)
        p = jnp.exp(s - m)
        p_new = jnp.exp(s_new - m)
        denom = jnp.sum(p, axis=-1, keepdims=True) + p_new + jnp.exp(sink_col - m)
        pv = _dot_nt(p.astype(BF16), vs_ref[j].astype(BF16)) + p_new.astype(BF16).astype(F32) * vn
        ox_sc[b] = jnp.where(head_seg, pv / denom, 0.0)
        return carry

    lax.fori_loop(0, SAMPLE_BLOCK, one_sample, 0, unroll=4)

    @pl.when(i == nsteps - 1)
    def _():
        groups = []
        for gi in range(GROUP):
            og = ox_sc[:, gi, :]
            for k in range(1, N_KV_HEADS):
                og = og + ox_sc[:, k * GROUP + gi, :]
            groups.append(og.astype(BF16))
        o_ref[...] = x_ref[...] + _dot(jnp.concatenate(groups, axis=1), wo_ref[...])


def _attn_sample(x, k_state, v_state, k_new, v_new, slopes, sinks, layer, weights):
    n, buf_len = k_state.shape[0], k_state.shape[2]
    nsteps = n // SAMPLE_BLOCK
    st_spec = pl.BlockSpec((SAMPLE_BLOCK, KV_DIM, buf_len), lambda i: (i, 0, 0))
    smem = pl.BlockSpec(memory_space=pltpu.SMEM)
    return pl.pallas_call(
        functools.partial(_attn_sample_kernel, nsteps=nsteps, buf_len=buf_len, layer=layer),
        grid=(nsteps,),
        in_specs=[smem, smem,
                  _const_spec((n, D_MODEL)),
                  st_spec, st_spec,
                  _const_spec((n, KV_DIM)),
                  _const_spec((n, KV_DIM))] + _attn_specs(layer),
        out_specs=pl.BlockSpec((n, D_MODEL), lambda i: (0, 0)),
        out_shape=jax.ShapeDtypeStruct((n, D_MODEL), F32),
        scratch_shapes=[pltpu.VMEM((n, N_HEADS, KV_DIM), F32),
                        pltpu.VMEM((n, N_HEADS, KV_DIM), F32)],
        compiler_params=_params(),
        name="attn_sample",
    )(slopes, sinks, x, k_state, v_state, k_new, v_new, *weights)


def _window_shift_kernel(ks_ref, vs_ref, kn_ref, vn_ref, ko_ref, vo_ref, knt_sc, vnt_sc, *, buf_len):
    i = pl.program_id(0)

    @pl.when(i == 0)
    def _():
        knt_sc[...] = kn_ref[...].T
        vnt_sc[...] = vn_ref[...].T

    newest = lax.broadcasted_iota(jnp.int32, (KV_DIM, buf_len), 1) == buf_len - 1

    def one_sample(j, carry):
        b = i * SAMPLE_BLOCK + j
        for s_ref, nt_sc, out_ref in ((ks_ref, knt_sc, ko_ref), (vs_ref, vnt_sc, vo_ref)):
            shifted = pltpu.roll(s_ref[j], buf_len - 1, 1)
            new_col = pltpu.roll(nt_sc[...], buf_len - 1 - b, 1)
            out_ref[j] = jnp.where(newest, new_col, shifted)
        return carry

    lax.fori_loop(0, SAMPLE_BLOCK, one_sample, 0)


def _window_shift(k_state, v_state, k_new, v_new):
    n, buf_len = k_state.shape[0], k_state.shape[2]
    assert n == buf_len
    st_spec = pl.BlockSpec((SAMPLE_BLOCK, KV_DIM, buf_len), lambda i: (i, 0, 0))
    new_spec = _const_spec((n, KV_DIM))
    return pl.pallas_call(
        functools.partial(_window_shift_kernel, buf_len=buf_len),
        grid=(n // SAMPLE_BLOCK,),
        in_specs=[st_spec, st_spec, new_spec, new_spec],
        out_specs=[st_spec, st_spec],
        out_shape=[jax.ShapeDtypeStruct(k_state.shape, F32)] * 2,
        scratch_shapes=[pltpu.VMEM((KV_DIM, n), F32)] * 2,
        compiler_params=_params(),
        name="window_shift",
    )(k_state, v_state, k_new, v_new)


def _regroup_cols(w):
    n = w.shape[0]
    w = w.reshape(n, D_MODEL, N_KV_HEADS, GROUP, HEAD_DIM).transpose(0, 1, 3, 2, 4)
    return w.reshape(n, D_MODEL, D_MODEL)


def _regroup_rows(w):
    n = w.shape[0]
    w = w.reshape(n, N_KV_HEADS, GROUP, HEAD_DIM, D_MODEL).transpose(0, 2, 1, 3, 4)
    return w.reshape(n, D_MODEL, D_MODEL)


def kernel(x_prompt, x_sample, state_conv, state_win_k, state_win_v, norm_mix_g, norm_mlp_g, conv_w1, conv_b1, conv_dw, conv_dwb, conv_ln_g, conv_ln_b, conv_w2, conv_b2, kv_norm_g, w_kv, k_norm_g, attn_wq, q_norm_g, attn_sinks, attn_wo, mlp_w1, mlp_w2):
    bp, sp, _ = x_prompt.shape
    bs = x_sample.shape[0]
    n_a = conv_w1.shape[0]
    depth = mlp_w1.shape[0]
    buf_len = state_win_k.shape[1]
    assert x_sample.shape[1] == 1 and sp % TM == 0 and buf_len == WINDOW and bs % SAMPLE_BLOCK == 0

    xp = x_prompt.reshape(bp * sp, D_MODEL)
    xs = x_sample.reshape(bs, D_MODEL)
    rows3 = lambda v: v.reshape(v.shape[0], 1, v.shape[-1])
    slopes = jnp.exp2(-8.0 * jnp.arange(1, N_HEADS + 1, dtype=F32) / N_HEADS)
    head_gain = lambda gvec: jnp.tile(gvec, (1, N_KV_HEADS)).reshape(gvec.shape[0], 1, KV_DIM)

    mix_g, mlp_g = rows3(norm_mix_g), rows3(norm_mlp_g)
    conv_weights = (mix_g, conv_w1.astype(BF16), rows3(conv_b1), conv_dw, rows3(conv_dwb),
                    rows3(conv_ln_g), rows3(conv_ln_b), conv_w2.astype(BF16), rows3(conv_b2))
    mlp_weights = (mlp_g, mlp_w1.astype(BF16), mlp_w2.astype(BF16))
    kv_weights = (kv_norm_g.reshape(1, D_MODEL), w_kv.astype(BF16), jnp.tile(k_norm_g, N_KV_HEADS).reshape(1, KV_DIM))
    attn_weights = (mix_g[n_a:], _regroup_cols(attn_wq).astype(BF16), head_gain(q_norm_g),
                    _regroup_rows(attn_wo).astype(BF16))

    conv_state = state_conv.transpose(0, 2, 1, 3)
    k_state = state_win_k.transpose(0, 2, 3, 1).reshape(bs, KV_DIM, buf_len)
    v_state = state_win_v.transpose(0, 2, 3, 1).reshape(bs, KV_DIM, buf_len)

    conv_p, conv_u = [], []
    for l in range(depth):
        if l < n_a:
            if l == n_a - 1:
                xp, carry, kb, vb, kf, vf = _conv_layer_prompt(xp, bp, l, conv_weights, mlp_weights,
                                                               kv_weights, keep=min(WINDOW, sp))
            else:
                xp, carry = _conv_layer_prompt(xp, bp, l, conv_weights, mlp_weights)
            xs, u = _amix_sample(xs, conv_state, l, conv_weights)
            conv_p.append(carry[:, CARRY - CONV_PREV:, :])
            conv_u.append(u)
        else:
            if l == n_a:
                _, _, kn, vn = _kv_proj(xs, 1, *kv_weights, tm=bs, keep=bs)
                new_ks, new_vs = _window_shift(k_state, v_state, kn, vn)
            bl = l - n_a
            xp = _attn_prompt(xp, bp, kb, vb, slopes, attn_sinks, bl, attn_weights)
            xs = _attn_sample(xs, k_state, v_state, kn, vn, slopes, attn_sinks, bl, attn_weights)
            xp = _mlp(xp, l, *mlp_weights, tm=TM)
        xs = _mlp(xs, l, *mlp_weights, tm=bs)

    new_conv_s = _conv_state(conv_state, jnp.stack(conv_u, axis=0)).transpose(0, 2, 1, 3)
    keep = min(WINDOW, sp)
    kv_shape = (N_KV_HEADS, HEAD_DIM)
    window_out = lambda w: w.reshape(bs, N_KV_HEADS, HEAD_DIM, buf_len).transpose(0, 3, 1, 2)
    return (xp.reshape(bp, sp, D_MODEL), xs.reshape(bs, 1, D_MODEL),
            jnp.stack(conv_p, axis=0), new_conv_s,
            kf.reshape(bp, keep, *kv_shape), vf.reshape(bp, keep, *kv_shape),
            window_out(new_ks), window_out(new_vs))
```

```python
import functools

import jax
import jax.numpy as jnp
from jax import lax
from jax.experimental import pallas as pl
from jax.experimental.pallas import tpu as pltpu

F32 = jnp.float32
BF16 = jnp.bfloat16

D_MODEL = 1024
N_HEADS = 16
HEAD_DIM = 64
N_KV_HEADS = 4
GROUP = 4
KV_DIM = N_KV_HEADS * HEAD_DIM
WINDOW = 128
BLOCK = 128
CONV_WIDTH = 31
CONV_PREV = CONV_WIDTH - 1
CARRY = 32
D_FF = 4 * D_MODEL
RMS_EPS = 1e-6
LN_EPS = 1e-5

SUBLANES = 8
PACKED_ROWS = 16
LANES = 128
TM = 512
FF_CHUNK = 1024
STATE_COLS = 256
SAMPLE_BLOCK = 32
VMEM_LIMIT = 56 * 1024 * 1024


def _const_spec(shape):
    nd = len(shape)
    return pl.BlockSpec(shape, lambda *_: (0,) * nd, pipeline_mode=pl.Buffered(1))


def _layer_spec(shape, layer):
    nd = len(shape)
    return pl.BlockSpec((None,) + tuple(shape), lambda *_: (layer,) + (0,) * nd, pipeline_mode=pl.Buffered(1))


def _params(n_axes=1):
    return pltpu.CompilerParams(dimension_semantics=("arbitrary",) * n_axes, vmem_limit_bytes=VMEM_LIMIT)


def _rms(x, g):
    return x * lax.rsqrt(jnp.mean(x * x, axis=-1, keepdims=True) + RMS_EPS) * g


def _dot(a, b):
    return jnp.dot(a, b, preferred_element_type=F32)


def _dot_nt(a, b):
    return lax.dot_general(a, b, (((1,), (1,)), ((), ())), preferred_element_type=F32)


def _segment_ones():
    r = lax.broadcasted_iota(jnp.int32, (KV_DIM, KV_DIM), 0) // HEAD_DIM
    c = lax.broadcasted_iota(jnp.int32, (KV_DIM, KV_DIM), 1) // HEAD_DIM
    return (r == c).astype(BF16)


def _head_rms(z, g, seg):
    s = z * z
    hi = s.astype(BF16)
    lo = (s - hi.astype(F32)).astype(BF16)
    ms = (_dot(hi, seg) + _dot(lo, seg)) * (1.0 / HEAD_DIM)
    return z * lax.rsqrt(ms + RMS_EPS) * g


def _mlp_kernel(x_ref, g_ref, w1_ref, w2_ref, o_ref):
    x = x_ref[...]
    h = _rms(x, g_ref[...]).astype(BF16)
    acc = x
    for c in range(D_FF // FF_CHUNK):
        a = _dot(h, w1_ref[:, c * FF_CHUNK:(c + 1) * FF_CHUNK])
        a = jnp.maximum(a, 0.0)
        a = (a * a).astype(BF16)
        acc = acc + _dot(a, w2_ref[c * FF_CHUNK:(c + 1) * FF_CHUNK, :])
    o_ref[...] = acc


def _mlp(x, layer, g, w1, w2, tm):
    rows = x.shape[0]
    return pl.pallas_call(
        _mlp_kernel,
        grid=(rows // tm,),
        in_specs=[pl.BlockSpec((tm, D_MODEL), lambda i: (i, 0)),
                  _layer_spec((1, D_MODEL), layer),
                  _layer_spec((D_MODEL, D_FF), layer),
                  _layer_spec((D_FF, D_MODEL), layer)],
        out_specs=pl.BlockSpec((tm, D_MODEL), lambda i: (i, 0)),
        out_shape=jax.ShapeDtypeStruct((rows, D_MODEL), F32),
        compiler_params=_params(),
        name="mlp",
    )(x, g, w1, w2)


def _glu_in(x, g, w1_ref, b1_ref):
    h = _rms(x, g).astype(BF16)
    a = _dot(h, w1_ref[...]) + b1_ref[...]
    return a[:, :D_MODEL] * jax.nn.sigmoid(a[:, D_MODEL:])


def _conv_out(x, c, lng, lnb, w2_ref, b2_ref):
    mu = jnp.mean(c, axis=-1, keepdims=True)
    d = c - mu
    var = jnp.mean(d * d, axis=-1, keepdims=True)
    y = d * lax.rsqrt(var + LN_EPS) * lng + lnb
    y = y * jax.nn.sigmoid(y)
    return x + _dot(y.astype(BF16), w2_ref[...]) + b2_ref[...]


def _amix_specs(layer):
    return [_layer_spec((1, D_MODEL), layer),
            _layer_spec((D_MODEL, 2 * D_MODEL), layer),
            _layer_spec((1, 2 * D_MODEL), layer),
            _layer_spec((CONV_WIDTH, D_MODEL), layer),
            _layer_spec((1, D_MODEL), layer),
            _layer_spec((1, D_MODEL), layer),
            _layer_spec((1, D_MODEL), layer),
            _layer_spec((D_MODEL, D_MODEL), layer),
            _layer_spec((1, D_MODEL), layer)]


def _amix_prompt_kernel(x_ref, g_ref, w1_ref, b1_ref, dw_ref, dwb_ref, lng_ref, lnb_ref, w2_ref, b2_ref, dwp_ref,
                        o_ref, carry_ref, ubuf, esh, cbuf, *, nt):
    del dw_ref
    t = pl.program_id(0) % nt

    @pl.when(t == 0)
    def _():
        ubuf[0:CARRY, :] = jnp.zeros((CARRY, D_MODEL), F32)

    x = x_ref[...]
    ubuf[CARRY:CARRY + TM, :] = _glu_in(x, g_ref[...], w1_ref, b1_ref)

    first = CARRY - CONV_PREV
    shifted = TM + PACKED_ROWS
    for c in range(D_MODEL // LANES):
        cols = slice(c * LANES, (c + 1) * LANES)
        slab = ubuf[:, cols]
        esh[0] = slab.astype(BF16)
        esh[SUBLANES, 0:shifted] = slab[SUBLANES:SUBLANES + shifted].astype(BF16)
        for b in range(1, SUBLANES):
            sb = pltpu.roll(slab, CARRY + TM - b, 0)
            esh[b, 0:shifted] = sb[0:shifted].astype(BF16)
            esh[b + SUBLANES, 0:shifted] = sb[SUBLANES:SUBLANES + shifted].astype(BF16)

        @pl.when(t >= 0)
        def _():
            for r0 in range(0, TM, PACKED_ROWS):
                acc = jnp.zeros((PACKED_ROWS, LANES), F32)
                for o in range(first, CARRY + 1):
                    a, b = divmod(o, PACKED_ROWS)
                    src = r0 + a * PACKED_ROWS
                    acc = acc + esh[b, src:src + PACKED_ROWS, :].astype(F32) * dwp_ref[c, o - first].astype(F32)
                cbuf[r0:r0 + PACKED_ROWS, cols] = acc + dwb_ref[:, cols]

    tail = ubuf[TM:TM + CARRY, :]
    carry_ref[...] = tail
    ubuf[0:CARRY, :] = tail

    o_ref[...] = _conv_out(x, cbuf[...], lng_ref[...], lnb_ref[...], w2_ref, b2_ref)


def _amix_prompt(x, batch, layer, weights, dw_packed):
    rows = x.shape[0]
    nt = rows // batch // TM
    return pl.pallas_call(
        functools.partial(_amix_prompt_kernel, nt=nt),
        grid=(rows // TM,),
        in_specs=([pl.BlockSpec((TM, D_MODEL), lambda i: (i, 0))] + _amix_specs(layer)
                  + [_layer_spec((D_MODEL // LANES, CONV_WIDTH, PACKED_ROWS, LANES), layer)]),
        out_specs=[pl.BlockSpec((TM, D_MODEL), lambda i: (i, 0)),
                   pl.BlockSpec((None, CARRY, D_MODEL), lambda i: (i // nt, 0, 0))],
        out_shape=[jax.ShapeDtypeStruct((rows, D_MODEL), F32),
                   jax.ShapeDtypeStruct((batch, CARRY, D_MODEL), F32)],
        scratch_shapes=[pltpu.VMEM((CARRY + TM, D_MODEL), F32),
                        pltpu.VMEM((2 * SUBLANES, CARRY + TM, LANES), BF16),
                        pltpu.VMEM((TM, D_MODEL), F32)],
        compiler_params=_params(),
        name="amix_prompt",
    )(x, *weights, dw_packed)


def _amix_sample_kernel(x_ref, st_ref, g_ref, w1_ref, b1_ref, dw_ref, dwb_ref, lng_ref, lnb_ref, w2_ref, b2_ref,
                        o_ref, u_ref, c_sc, *, nsteps):
    i = pl.program_id(0)

    @pl.when(i == 0)
    def _():
        u_ref[...] = _glu_in(x_ref[...], g_ref[...], w1_ref, b1_ref)

    cols = pl.ds(pl.multiple_of(i * STATE_COLS, STATE_COLS), STATE_COLS)
    acc = dwb_ref[:, cols] + u_ref[:, cols] * dw_ref[CONV_PREV:CONV_WIDTH, cols]
    for j in range(CONV_PREV):
        acc = acc + st_ref[j] * dw_ref[j:j + 1, cols]
    c_sc[:, cols] = acc

    @pl.when(i == nsteps - 1)
    def _():
        o_ref[...] = _conv_out(x_ref[...], c_sc[...], lng_ref[...], lnb_ref[...], w2_ref, b2_ref)


def _amix_sample(x, state, layer, weights):
    n = x.shape[0]
    nsteps = D_MODEL // STATE_COLS
    return pl.pallas_call(
        functools.partial(_amix_sample_kernel, nsteps=nsteps),
        grid=(nsteps,),
        in_specs=[_const_spec((n, D_MODEL)),
                  pl.BlockSpec((None, CONV_PREV, n, STATE_COLS), lambda i: (layer, 0, 0, i))] + _amix_specs(layer),
        out_specs=[pl.BlockSpec((n, D_MODEL), lambda i: (0, 0)),
                   pl.BlockSpec((n, D_MODEL), lambda i: (0, 0))],
        out_shape=[jax.ShapeDtypeStruct((n, D_MODEL), F32),
                   jax.ShapeDtypeStruct((n, D_MODEL), F32)],
        scratch_shapes=[pltpu.VMEM((n, D_MODEL), F32)],
        compiler_params=_params(),
        name="amix_sample",
    )(x, state, *weights)


def _conv_state_kernel(st_ref, u_ref, o_ref):
    for j in range(1, CONV_PREV):
        o_ref[j - 1] = st_ref[j]
    o_ref[CONV_PREV - 1] = u_ref[...]


def _conv_state(state, u):
    n_layers, _, n, _ = state.shape
    st_spec = pl.BlockSpec((None, CONV_PREV, n, STATE_COLS), lambda a, c: (a, 0, 0, c))
    return pl.pallas_call(
        _conv_state_kernel,
        grid=(n_layers, D_MODEL // STATE_COLS),
        in_specs=[st_spec, pl.BlockSpec((None, n, STATE_COLS), lambda a, c: (a, 0, c))],
        out_specs=st_spec,
        out_shape=jax.ShapeDtypeStruct(state.shape, F32),
        compiler_params=_params(2),
        name="conv_state",
    )(state, u)


def _kv_kernel(x_ref, g_ref, w_ref, kg_ref, kb_ref, vb_ref, kf_ref, vf_ref, *, keep):
    h = _rms(x_ref[...], g_ref[...]).astype(BF16)
    kv = _dot(h, w_ref[...])
    k = _head_rms(kv[:, :KV_DIM], kg_ref[...], _segment_ones())
    v = kv[:, KV_DIM:]
    kb_ref[...] = k.astype(BF16)
    vb_ref[...] = v.astype(BF16)
    tm = k.shape[0]
    kf_ref[...] = k[tm - keep:, :]
    vf_ref[...] = v[tm - keep:, :]


def _kv_proj(x, groups, g, w, kg, tm, keep):
    rows = x.shape[0]
    nt = rows // groups // tm
    row_spec = pl.BlockSpec((tm, KV_DIM), lambda i: (i, 0))
    keep_spec = pl.BlockSpec((keep, KV_DIM), lambda i: (i // nt, 0))
    return pl.pallas_call(
        functools.partial(_kv_kernel, keep=keep),
        grid=(rows // tm,),
        in_specs=[pl.BlockSpec((tm, D_MODEL), lambda i: (i, 0)),
                  _const_spec((1, D_MODEL)),
                  _const_spec((D_MODEL, 2 * KV_DIM)),
                  _const_spec((1, KV_DIM))],
        out_specs=[row_spec, row_spec, keep_spec, keep_spec],
        out_shape=[jax.ShapeDtypeStruct((rows, KV_DIM), BF16),
                   jax.ShapeDtypeStruct((rows, KV_DIM), BF16),
                   jax.ShapeDtypeStruct((groups * keep, KV_DIM), F32),
                   jax.ShapeDtypeStruct((groups * keep, KV_DIM), F32)],
        compiler_params=_params(),
        name="kv_proj",
    )(x, g, w, kg)


def _queries(x, g, wq_ref, qg, seg):
    h = _rms(x, g).astype(BF16)
    q = _dot(h, wq_ref[...])
    out = []
    for gi in range(GROUP):
        qn = _head_rms(q[:, gi * KV_DIM:(gi + 1) * KV_DIM], qg, seg)
        out.append(qn * (HEAD_DIM ** -0.5))
    return out


def _lane_segment(k, rows):
    lane = lax.broadcasted_iota(jnp.int32, (rows, KV_DIM), 1)
    return (lane >= k * HEAD_DIM) & (lane < (k + 1) * HEAD_DIM)


def _attn_specs(layer):
    return [_layer_spec((1, D_MODEL), layer),
            _layer_spec((D_MODEL, D_MODEL), layer),
            _layer_spec((1, KV_DIM), layer),
            _layer_spec((D_MODEL, D_MODEL), layer)]


def _attn_prompt_kernel(slopes_ref, sinks_ref, x_ref, kp_ref, kc_ref, vp_ref, vc_ref, g_ref, wq_ref, qg_ref, wo_ref,
                        o_ref, bias_sc, attn_sc, *, nt, layer):
    i = pl.program_id(0)
    t = i % nt
    delta = (BLOCK + lax.broadcasted_iota(jnp.int32, (BLOCK, 2 * BLOCK), 0)
             - lax.broadcasted_iota(jnp.int32, (BLOCK, 2 * BLOCK), 1))
    band = (delta >= 0) & (delta <= WINDOW)

    @pl.when(i == 0)
    def _():
        df = delta.astype(F32)
        for hd in range(N_HEADS):
            bias_sc[hd] = jnp.where(band, -slopes_ref[hd] * df, -jnp.inf)

    x = x_ref[...]
    seg = _segment_ones()
    qs = _queries(x, g_ref[...], wq_ref, qg_ref[...], seg)
    kcur = kc_ref[...]
    vcur = vc_ref[...]
    col = lax.broadcasted_iota(jnp.int32, (GROUP * BLOCK, 2 * BLOCK), 1)
    no_prev = jnp.where((col < BLOCK) & (t == 0), -jnp.inf, 0.0)

    for qb in range(TM // BLOCK):
        rows = slice(qb * BLOCK, (qb + 1) * BLOCK)
        if qb == 0:
            kctx = jnp.concatenate([kp_ref[...], kcur[:BLOCK]], axis=0)
            vctx = jnp.concatenate([vp_ref[...], vcur[:BLOCK]], axis=0)
        else:
            kctx = kcur[(qb - 1) * BLOCK:(qb + 1) * BLOCK]
            vctx = vcur[(qb - 1) * BLOCK:(qb + 1) * BLOCK]
        for k in range(N_KV_HEADS):
            seg_k = _lane_segment(k, BLOCK)
            qm = jnp.concatenate([jnp.where(seg_k, qs[gi][rows], 0.0) for gi in range(GROUP)], axis=0)
            s = _dot_nt(qm.astype(BF16), kctx)
            s = s + bias_sc[k * GROUP:(k + 1) * GROUP].reshape(GROUP * BLOCK, 2 * BLOCK)
            if qb == 0:
                s = s + no_prev
            ps, inv = [], []
            for gi in range(GROUP):
                sg = s[gi * BLOCK:(gi + 1) * BLOCK]
                sink = sinks_ref[layer, k * GROUP + gi]
                m = jnp.maximum(jnp.max(sg, axis=-1, keepdims=True), sink)
                p = jnp.exp(sg - m)
                inv.append(1.0 / (jnp.sum(p, axis=-1, keepdims=True) + jnp.exp(sink - m)))
                ps.append(p.astype(BF16))
            pv = _dot(jnp.concatenate(ps, axis=0), vctx)
            for gi in range(GROUP):
                og = pv[gi * BLOCK:(gi + 1) * BLOCK] * inv[gi]
                lanes = slice(gi * KV_DIM + k * HEAD_DIM, gi * KV_DIM + (k + 1) * HEAD_DIM)
                attn_sc[rows, lanes] = og[:, k * HEAD_DIM:(k + 1) * HEAD_DIM]

    o_ref[...] = x + _dot(attn_sc[...].astype(BF16), wo_ref[...])


def _attn_prompt(x, batch, kb, vb, slopes, sinks, layer, weights):
    rows = x.shape[0]
    nt = rows // batch // TM
    per = TM // BLOCK
    prev_spec = pl.BlockSpec((BLOCK, KV_DIM), lambda i: (jnp.maximum(i * per - 1, 0), 0))
    cur_spec = pl.BlockSpec((TM, KV_DIM), lambda i: (i, 0))
    smem = pl.BlockSpec(memory_space=pltpu.SMEM)
    return pl.pallas_call(
        functools.partial(_attn_prompt_kernel, nt=nt, layer=layer),
        grid=(rows // TM,),
        in_specs=[smem, smem,
                  pl.BlockSpec((TM, D_MODEL), lambda i: (i, 0)),
                  prev_spec, cur_spec, prev_spec, cur_spec] + _attn_specs(layer),
        out_specs=pl.BlockSpec((TM, D_MODEL), lambda i: (i, 0)),
        out_shape=jax.ShapeDtypeStruct((rows, D_MODEL), F32),
        scratch_shapes=[pltpu.VMEM((N_HEADS, BLOCK, 2 * BLOCK), F32),
                        pltpu.VMEM((TM, D_MODEL), F32)],
        compiler_params=_params(),
        name="attn_prompt",
    )(slopes, sinks, x, kb, kb, vb, vb, *weights)


def _attn_sample_kernel(slopes_ref, sinks_ref, x_ref, ks_ref, vs_ref, kn_ref, vn_ref, g_ref, wq_ref, qg_ref, wo_ref,
                        o_ref, qx_sc, ox_sc, *, nsteps, buf_len, layer):
    i = pl.program_id(0)
    n = x_ref.shape[0]

    @pl.when(i == 0)
    def _():
        qs = _queries(x_ref[...], g_ref[...], wq_ref, qg_ref[...], _segment_ones())
        for k in range(N_KV_HEADS):
            seg_k = _lane_segment(k, n)
            for gi in range(GROUP):
                qx_sc[:, k * GROUP + gi, :] = jnp.where(seg_k, qs[gi], 0.0)

    hrow = lax.broadcasted_iota(jnp.int32, (N_HEADS, 1), 0)
    slope_col = jnp.zeros((N_HEADS, 1), F32)
    sink_col = jnp.zeros((N_HEADS, 1), F32)
    for hd in range(N_HEADS):
        slope_col = jnp.where(hrow == hd, slopes_ref[hd], slope_col)
        sink_col = jnp.where(hrow == hd, sinks_ref[layer, hd], sink_col)
    dist = (buf_len - lax.broadcasted_iota(jnp.int32, (N_HEADS, buf_len), 1)).astype(F32)
    bias = -slope_col * dist
    head_seg = (lax.broadcasted_iota(jnp.int32, (N_HEADS, KV_DIM), 0) // GROUP
                == lax.broadcasted_iota(jnp.int32, (N_HEADS, KV_DIM), 1) // HEAD_DIM)

    def one_sample(j, carry):
        b = i * SAMPLE_BLOCK + j
        qx = qx_sc[b].astype(BF16)
        kn = kn_ref[pl.ds(b, 1), :].astype(BF16).astype(F32)
        vn = vn_ref[pl.ds(b, 1), :].astype(BF16).astype(F32)
        s = _dot(qx, ks_ref[j].astype(BF16)) + bias
        s_new = jnp.sum(qx.astype(F32) * kn, axis=-1, keepdims=True)
        m = jnp.maximum(jnp.maximum(jnp.max(s, axis=-1, keepdims=True), s_new), sink_col)
        p = jnp.exp(s - m)
        p_new = jnp.exp(s_new - m)
        denom = jnp.sum(p, axis=-1, keepdims=True) + p_new + jnp.exp(sink_col - m)
        pv = _dot_nt(p.astype(BF16), vs_ref[j].astype(BF16)) + p_new.astype(BF16).astype(F32) * vn
        ox_sc[b] = jnp.where(head_seg, pv / denom, 0.0)
        return carry

    lax.fori_loop(0, SAMPLE_BLOCK, one_sample, 0, unroll=4)

    @pl.when(i == nsteps - 1)
    def _():
        groups = []
        for gi in range(GROUP):
            og = ox_sc[:, gi, :]
            for k in range(1, N_KV_HEADS):
                og = og + ox_sc[:, k * GROUP + gi, :]
            groups.append(og.astype(BF16))
        o_ref[...] = x_ref[...] + _dot(jnp.concatenate(groups, axis=1), wo_ref[...])


def _attn_sample(x, k_state, v_state, k_new, v_new, slopes, sinks, layer, weights):
    n, buf_len = k_state.shape[0], k_state.shape[2]
    nsteps = n // SAMPLE_BLOCK
    st_spec = pl.BlockSpec((SAMPLE_BLOCK, KV_DIM, buf_len), lambda i: (i, 0, 0))
    smem = pl.BlockSpec(memory_space=pltpu.SMEM)
    return pl.pallas_call(
        functools.partial(_attn_sample_kernel, nsteps=nsteps, buf_len=buf_len, layer=layer),
        grid=(nsteps,),
        in_specs=[smem, smem,
                  _const_spec((n, D_MODEL)),
                  st_spec, st_spec,
                  _const_spec((n, KV_DIM)),
                  _const_spec((n, KV_DIM))] + _attn_specs(layer),
        out_specs=pl.BlockSpec((n, D_MODEL), lambda i: (0, 0)),
        out_shape=jax.ShapeDtypeStruct((n, D_MODEL), F32),
        scratch_shapes=[pltpu.VMEM((n, N_HEADS, KV_DIM), F32),
                        pltpu.VMEM((n, N_HEADS, KV_DIM), F32)],
        compiler_params=_params(),
        name="attn_sample",
    )(slopes, sinks, x, k_state, v_state, k_new, v_new, *weights)


def _window_shift_kernel(ks_ref, vs_ref, kn_ref, vn_ref, ko_ref, vo_ref, knt_sc, vnt_sc, *, buf_len):
    i = pl.program_id(0)

    @pl.when(i == 0)
    def _():
        knt_sc[...] = kn_ref[...].T
        vnt_sc[...] = vn_ref[...].T

    newest = lax.broadcasted_iota(jnp.int32, (KV_DIM, buf_len), 1) == buf_len - 1

    def one_sample(j, carry):
        b = i * SAMPLE_BLOCK + j
        for s_ref, nt_sc, out_ref in ((ks_ref, knt_sc, ko_ref), (vs_ref, vnt_sc, vo_ref)):
            shifted = pltpu.roll(s_ref[j], buf_len - 1, 1)
            new_col = pltpu.roll(nt_sc[...], buf_len - 1 - b, 1)
            out_ref[j] = jnp.where(newest, new_col, shifted)
        return carry

    lax.fori_loop(0, SAMPLE_BLOCK, one_sample, 0)


def _window_shift(k_state, v_state, k_new, v_new):
    n, buf_len = k_state.shape[0], k_state.shape[2]
    assert n == buf_len
    st_spec = pl.BlockSpec((SAMPLE_BLOCK, KV_DIM, buf_len), lambda i: (i, 0, 0))
    new_spec = _const_spec((n, KV_DIM))
    return pl.pallas_call(
        functools.partial(_window_shift_kernel, buf_len=buf_len),
        grid=(n // SAMPLE_BLOCK,),
        in_specs=[st_spec, st_spec, new_spec, new_spec],
        out_specs=[st_spec, st_spec],
        out_shape=[jax.ShapeDtypeStruct(k_state.shape, F32)] * 2,
        scratch_shapes=[pltpu.VMEM((KV_DIM, n), F32)] * 2,
        compiler_params=_params(),
        name="window_shift",
    )(k_state, v_state, k_new, v_new)


def _regroup_cols(w):
    n = w.shape[0]
    w = w.reshape(n, D_MODEL, N_KV_HEADS, GROUP, HEAD_DIM).transpose(0, 1, 3, 2, 4)
    return w.reshape(n, D_MODEL, D_MODEL)


def _regroup_rows(w):
    n = w.shape[0]
    w = w.reshape(n, N_KV_HEADS, GROUP, HEAD_DIM, D_MODEL).transpose(0, 2, 1, 3, 4)
    return w.reshape(n, D_MODEL, D_MODEL)


def kernel(x_prompt, x_sample, state_conv, state_win_k, state_win_v, norm_mix_g, norm_mlp_g, conv_w1, conv_b1, conv_dw, conv_dwb, conv_ln_g, conv_ln_b, conv_w2, conv_b2, kv_norm_g, w_kv, k_norm_g, attn_wq, q_norm_g, attn_sinks, attn_wo, mlp_w1, mlp_w2):
    bp, sp, _ = x_prompt.shape
    bs = x_sample.shape[0]
    n_a = conv_w1.shape[0]
    depth = mlp_w1.shape[0]
    buf_len = state_win_k.shape[1]
    assert x_sample.shape[1] == 1 and sp % TM == 0 and buf_len == WINDOW and bs % SAMPLE_BLOCK == 0

    xp = x_prompt.reshape(bp * sp, D_MODEL)
    xs = x_sample.reshape(bs, D_MODEL)
    rows3 = lambda v: v.reshape(v.shape[0], 1, v.shape[-1])
    slopes = jnp.exp2(-8.0 * jnp.arange(1, N_HEADS + 1, dtype=F32) / N_HEADS)
    head_gain = lambda gvec: jnp.tile(gvec, (1, N_KV_HEADS)).reshape(gvec.shape[0], 1, KV_DIM)

    mix_g, mlp_g = rows3(norm_mix_g), rows3(norm_mlp_g)
    conv_weights = (mix_g, conv_w1.astype(BF16), rows3(conv_b1), conv_dw, rows3(conv_dwb),
                    rows3(conv_ln_g), rows3(conv_ln_b), conv_w2.astype(BF16), rows3(conv_b2))
    dw_slabs = conv_dw.astype(BF16).reshape(n_a, CONV_WIDTH, D_MODEL // LANES, LANES).transpose(0, 2, 1, 3)
    dw_packed = jnp.broadcast_to(dw_slabs[:, :, :, None, :], (n_a, D_MODEL // LANES, CONV_WIDTH, PACKED_ROWS, LANES))
    mlp_weights = (mlp_g, mlp_w1.astype(BF16), mlp_w2.astype(BF16))
    kv_weights = (kv_norm_g.reshape(1, D_MODEL), w_kv.astype(BF16), jnp.tile(k_norm_g, N_KV_HEADS).reshape(1, KV_DIM))
    attn_weights = (mix_g[n_a:], _regroup_cols(attn_wq).astype(BF16), head_gain(q_norm_g),
                    _regroup_rows(attn_wo).astype(BF16))

    conv_state = state_conv.transpose(0, 2, 1, 3)
    k_state = state_win_k.transpose(0, 2, 3, 1).reshape(bs, KV_DIM, buf_len)
    v_state = state_win_v.transpose(0, 2, 3, 1).reshape(bs, KV_DIM, buf_len)

    conv_p, conv_u = [], []
    for l in range(depth):
        if l < n_a:
            xp, carry = _amix_prompt(xp, bp, l, conv_weights, dw_packed)
            xs, u = _amix_sample(xs, conv_state, l, conv_weights)
            conv_p.append(carry[:, CARRY - CONV_PREV:, :])
            conv_u.append(u)
        else:
            if l == n_a:
                kb, vb, kf, vf = _kv_proj(xp, bp, *kv_weights, tm=TM, keep=min(WINDOW, sp))
                _, _, kn, vn = _kv_proj(xs, 1, *kv_weights, tm=bs, keep=bs)
                new_ks, new_vs = _window_shift(k_state, v_state, kn, vn)
            bl = l - n_a
            xp = _attn_prompt(xp, bp, kb, vb, slopes, attn_sinks, bl, attn_weights)
            xs = _attn_sample(xs, k_state, v_state, kn, vn, slopes, attn_sinks, bl, attn_weights)
        xp = _mlp(xp, l, *mlp_weights, tm=TM)
        xs = _mlp(xs, l, *mlp_weights, tm=bs)

    new_conv_s = _conv_state(conv_state, jnp.stack(conv_u, axis=0)).transpose(0, 2, 1, 3)
    keep = min(WINDOW, sp)
    kv_shape = (N_KV_HEADS, HEAD_DIM)
    window_out = lambda w: w.reshape(bs, N_KV_HEADS, HEAD_DIM, buf_len).transpose(0, 3, 1, 2)
    return (xp.reshape(bp, sp, D_MODEL), xs.reshape(bs, 1, D_MODEL),
            jnp.stack(conv_p, axis=0), new_conv_s,
            kf.reshape(bp, keep, *kv_shape), vf.reshape(bp, keep, *kv_shape),
            window_out(new_ks), window_out(new_vs))
```

```python
import functools

import jax
import jax.numpy as jnp
from jax import lax
from jax.experimental import pallas as pl
from jax.experimental.pallas import tpu as pltpu

F32 = jnp.float32
BF16 = jnp.bfloat16

D_MODEL = 1024
N_HEADS = 16
HEAD_DIM = 64
N_KV_HEADS = 4
GROUP = 4
KV_DIM = N_KV_HEADS * HEAD_DIM
WINDOW = 128
BLOCK = 128
CONV_WIDTH = 31
CONV_PREV = CONV_WIDTH - 1
CARRY = 32
D_FF = 4 * D_MODEL
RMS_EPS = 1e-6
LN_EPS = 1e-5

SUBLANES = 8
PACKED_ROWS = 16
LANES = 128
TM = 512
MLP_TM = 1024
FF_CHUNK = 1024
STATE_COLS = 256
SAMPLE_BLOCK = 32
VMEM_LIMIT = 56 * 1024 * 1024


def _const_spec(shape):
    nd = len(shape)
    return pl.BlockSpec(shape, lambda *_: (0,) * nd, pipeline_mode=pl.Buffered(1))


def _layer_spec(shape, layer):
    nd = len(shape)
    return pl.BlockSpec((None,) + tuple(shape), lambda *_: (layer,) + (0,) * nd, pipeline_mode=pl.Buffered(1))


def _params(n_axes=1):
    return pltpu.CompilerParams(dimension_semantics=("arbitrary",) * n_axes, vmem_limit_bytes=VMEM_LIMIT)


def _rms(x, g):
    return x * lax.rsqrt(jnp.mean(x * x, axis=-1, keepdims=True) + RMS_EPS) * g


def _dot(a, b):
    return jnp.dot(a, b, preferred_element_type=F32)


def _dot_nt(a, b):
    return lax.dot_general(a, b, (((1,), (1,)), ((), ())), preferred_element_type=F32)


def _segment_ones():
    r = lax.broadcasted_iota(jnp.int32, (KV_DIM, KV_DIM), 0) // HEAD_DIM
    c = lax.broadcasted_iota(jnp.int32, (KV_DIM, KV_DIM), 1) // HEAD_DIM
    return (r == c).astype(BF16)


def _head_rms(z, g, seg):
    s = z * z
    hi = s.astype(BF16)
    lo = (s - hi.astype(F32)).astype(BF16)
    ms = (_dot(hi, seg) + _dot(lo, seg)) * (1.0 / HEAD_DIM)
    return z * lax.rsqrt(ms + RMS_EPS) * g


def _mlp_rows(x_ref, g_ref, w1_ref, w2_ref, o_ref):
    x = x_ref[...]
    h = _rms(x, g_ref[...]).astype(BF16)
    acc = x
    for c in range(D_FF // FF_CHUNK):
        a = _dot(h, w1_ref[:, c * FF_CHUNK:(c + 1) * FF_CHUNK])
        a = jnp.maximum(a, 0.0)
        a = (a * a).astype(BF16)
        acc = acc + _dot(a, w2_ref[c * FF_CHUNK:(c + 1) * FF_CHUNK, :])
    o_ref[...] = acc


def _mlp_both_kernel(xs_ref, xp_ref, g_ref, w1_ref, w2_ref, os_ref, op_ref):
    i = pl.program_id(0)

    @pl.when(i == 0)
    def _():
        _mlp_rows(xs_ref, g_ref, w1_ref, w2_ref, os_ref)

    @pl.when(i > 0)
    def _():
        _mlp_rows(xp_ref, g_ref, w1_ref, w2_ref, op_ref)


def _mlp_both(xs, xp, layer, g, w1, w2):
    n, rows = xs.shape[0], xp.shape[0]
    prompt_tile = lambda i: (jnp.maximum(i - 1, 0), 0)
    return pl.pallas_call(
        _mlp_both_kernel,
        grid=(1 + rows // MLP_TM,),
        in_specs=[_const_spec((n, D_MODEL)),
                  pl.BlockSpec((MLP_TM, D_MODEL), prompt_tile),
                  _layer_spec((1, D_MODEL), layer),
                  _layer_spec((D_MODEL, D_FF), layer),
                  _layer_spec((D_FF, D_MODEL), layer)],
        out_specs=[pl.BlockSpec((n, D_MODEL), lambda i: (0, 0)),
                   pl.BlockSpec((MLP_TM, D_MODEL), prompt_tile)],
        out_shape=[jax.ShapeDtypeStruct((n, D_MODEL), F32),
                   jax.ShapeDtypeStruct((rows, D_MODEL), F32)],
        compiler_params=_params(),
        name="mlp_both",
    )(xs, xp, g, w1, w2)


def _glu_in(x, g, w1_ref, b1_ref):
    h = _rms(x, g).astype(BF16)
    a = _dot(h, w1_ref[...]) + b1_ref[...]
    return a[:, :D_MODEL] * jax.nn.sigmoid(a[:, D_MODEL:])


def _conv_out(x, c, lng, lnb, w2_ref, b2_ref):
    mu = jnp.mean(c, axis=-1, keepdims=True)
    d = c - mu
    var = jnp.mean(d * d, axis=-1, keepdims=True)
    y = d * lax.rsqrt(var + LN_EPS) * lng + lnb
    y = y * jax.nn.sigmoid(y)
    return x + _dot(y.astype(BF16), w2_ref[...]) + b2_ref[...]


def _amix_specs(layer):
    return [_layer_spec((1, D_MODEL), layer),
            _layer_spec((D_MODEL, 2 * D_MODEL), layer),
            _layer_spec((1, 2 * D_MODEL), layer),
            _layer_spec((CONV_WIDTH, D_MODEL), layer),
            _layer_spec((1, D_MODEL), layer),
            _layer_spec((1, D_MODEL), layer),
            _layer_spec((1, D_MODEL), layer),
            _layer_spec((D_MODEL, D_MODEL), layer),
            _layer_spec((1, D_MODEL), layer)]


def _amix_prompt_kernel(x_ref, g_ref, w1_ref, b1_ref, dw_ref, dwb_ref, lng_ref, lnb_ref, w2_ref, b2_ref, dwp_ref,
                        o_ref, carry_ref, ubuf, esh, cbuf, *, nt):
    del dw_ref
    t = pl.program_id(0) % nt

    @pl.when(t == 0)
    def _():
        ubuf[0:CARRY, :] = jnp.zeros((CARRY, D_MODEL), F32)

    x = x_ref[...]
    ubuf[CARRY:CARRY + TM, :] = _glu_in(x, g_ref[...], w1_ref, b1_ref)

    first = CARRY - CONV_PREV
    shifted = TM + PACKED_ROWS
    for c in range(D_MODEL // LANES):
        cols = slice(c * LANES, (c + 1) * LANES)
        slab = ubuf[:, cols]
        esh[0] = slab.astype(BF16)
        esh[SUBLANES, 0:shifted] = slab[SUBLANES:SUBLANES + shifted].astype(BF16)
        for b in range(1, SUBLANES):
            sb = pltpu.roll(slab, CARRY + TM - b, 0)
            esh[b, 0:shifted] = sb[0:shifted].astype(BF16)
            esh[b + SUBLANES, 0:shifted] = sb[SUBLANES:SUBLANES + shifted].astype(BF16)

        @pl.when(t >= 0)
        def _():
            for r0 in range(0, TM, PACKED_ROWS):
                acc = jnp.zeros((PACKED_ROWS, LANES), F32)
                for o in range(first, CARRY + 1):
                    a, b = divmod(o, PACKED_ROWS)
                    src = r0 + a * PACKED_ROWS
                    acc = acc + esh[b, src:src + PACKED_ROWS, :].astype(F32) * dwp_ref[c, o - first].astype(F32)
                cbuf[r0:r0 + PACKED_ROWS, cols] = acc + dwb_ref[:, cols]

    tail = ubuf[TM:TM + CARRY, :]
    carry_ref[...] = tail
    ubuf[0:CARRY, :] = tail

    o_ref[...] = _conv_out(x, cbuf[...], lng_ref[...], lnb_ref[...], w2_ref, b2_ref)


def _amix_prompt(x, batch, layer, weights, dw_packed):
    rows = x.shape[0]
    nt = rows // batch // TM
    return pl.pallas_call(
        functools.partial(_amix_prompt_kernel, nt=nt),
        grid=(rows // TM,),
        in_specs=([pl.BlockSpec((TM, D_MODEL), lambda i: (i, 0))] + _amix_specs(layer)
                  + [_layer_spec((D_MODEL // LANES, CONV_WIDTH, PACKED_ROWS, LANES), layer)]),
        out_specs=[pl.BlockSpec((TM, D_MODEL), lambda i: (i, 0)),
                   pl.BlockSpec((None, CARRY, D_MODEL), lambda i: (i // nt, 0, 0))],
        out_shape=[jax.ShapeDtypeStruct((rows, D_MODEL), F32),
                   jax.ShapeDtypeStruct((batch, CARRY, D_MODEL), F32)],
        scratch_shapes=[pltpu.VMEM((CARRY + TM, D_MODEL), F32),
                        pltpu.VMEM((2 * SUBLANES, CARRY + TM, LANES), BF16),
                        pltpu.VMEM((TM, D_MODEL), F32)],
        compiler_params=_params(),
        name="amix_prompt",
    )(x, *weights, dw_packed)


def _amix_sample_kernel(x_ref, st_ref, g_ref, w1_ref, b1_ref, dw_ref, dwb_ref, lng_ref, lnb_ref, w2_ref, b2_ref,
                        o_ref, u_ref, c_sc, *, nsteps):
    i = pl.program_id(0)

    @pl.when(i == 0)
    def _():
        u_ref[...] = _glu_in(x_ref[...], g_ref[...], w1_ref, b1_ref)

    cols = pl.ds(pl.multiple_of(i * STATE_COLS, STATE_COLS), STATE_COLS)
    acc = dwb_ref[:, cols] + u_ref[:, cols] * dw_ref[CONV_PREV:CONV_WIDTH, cols]
    for j in range(CONV_PREV):
        acc = acc + st_ref[j] * dw_ref[j:j + 1, cols]
    c_sc[:, cols] = acc

    @pl.when(i == nsteps - 1)
    def _():
        o_ref[...] = _conv_out(x_ref[...], c_sc[...], lng_ref[...], lnb_ref[...], w2_ref, b2_ref)


def _amix_sample(x, state, layer, weights):
    n = x.shape[0]
    nsteps = D_MODEL // STATE_COLS
    return pl.pallas_call(
        functools.partial(_amix_sample_kernel, nsteps=nsteps),
        grid=(nsteps,),
        in_specs=[_const_spec((n, D_MODEL)),
                  pl.BlockSpec((None, CONV_PREV, n, STATE_COLS), lambda i: (layer, 0, 0, i))] + _amix_specs(layer),
        out_specs=[pl.BlockSpec((n, D_MODEL), lambda i: (0, 0)),
                   pl.BlockSpec((n, D_MODEL), lambda i: (0, 0))],
        out_shape=[jax.ShapeDtypeStruct((n, D_MODEL), F32),
                   jax.ShapeDtypeStruct((n, D_MODEL), F32)],
        scratch_shapes=[pltpu.VMEM((n, D_MODEL), F32)],
        compiler_params=_params(),
        name="amix_sample",
    )(x, state, *weights)


def _conv_state_kernel(st_ref, u_ref, o_ref):
    for j in range(1, CONV_PREV):
        o_ref[j - 1] = st_ref[j]
    o_ref[CONV_PREV - 1] = u_ref[...]


def _conv_state(state, u):
    n_layers, _, n, _ = state.shape
    st_spec = pl.BlockSpec((None, CONV_PREV, n, STATE_COLS), lambda a, c: (a, 0, 0, c))
    return pl.pallas_call(
        _conv_state_kernel,
        grid=(n_layers, D_MODEL // STATE_COLS),
        in_specs=[st_spec, pl.BlockSpec((None, n, STATE_COLS), lambda a, c: (a, 0, c))],
        out_specs=st_spec,
        out_shape=jax.ShapeDtypeStruct(state.shape, F32),
        compiler_params=_params(2),
        name="conv_state",
    )(state, u)


def _kv_kernel(x_ref, g_ref, w_ref, kg_ref, kb_ref, vb_ref, kf_ref, vf_ref, *, keep):
    h = _rms(x_ref[...], g_ref[...]).astype(BF16)
    kv = _dot(h, w_ref[...])
    k = _head_rms(kv[:, :KV_DIM], kg_ref[...], _segment_ones())
    v = kv[:, KV_DIM:]
    kb_ref[...] = k.astype(BF16)
    vb_ref[...] = v.astype(BF16)
    tm = k.shape[0]
    kf_ref[...] = k[tm - keep:, :]
    vf_ref[...] = v[tm - keep:, :]


def _kv_proj(x, groups, g, w, kg, tm, keep):
    rows = x.shape[0]
    nt = rows // groups // tm
    row_spec = pl.BlockSpec((tm, KV_DIM), lambda i: (i, 0))
    keep_spec = pl.BlockSpec((keep, KV_DIM), lambda i: (i // nt, 0))
    return pl.pallas_call(
        functools.partial(_kv_kernel, keep=keep),
        grid=(rows // tm,),
        in_specs=[pl.BlockSpec((tm, D_MODEL), lambda i: (i, 0)),
                  _const_spec((1, D_MODEL)),
                  _const_spec((D_MODEL, 2 * KV_DIM)),
                  _const_spec((1, KV_DIM))],
        out_specs=[row_spec, row_spec, keep_spec, keep_spec],
        out_shape=[jax.ShapeDtypeStruct((rows, KV_DIM), BF16),
                   jax.ShapeDtypeStruct((rows, KV_DIM), BF16),
                   jax.ShapeDtypeStruct((groups * keep, KV_DIM), F32),
                   jax.ShapeDtypeStruct((groups * keep, KV_DIM), F32)],
        compiler_params=_params(),
        name="kv_proj",
    )(x, g, w, kg)


def _queries(x, g, wq_ref, qg, seg):
    h = _rms(x, g).astype(BF16)
    q = _dot(h, wq_ref[...])
    out = []
    for gi in range(GROUP):
        qn = _head_rms(q[:, gi * KV_DIM:(gi + 1) * KV_DIM], qg, seg)
        out.append(qn * (HEAD_DIM ** -0.5))
    return out


def _lane_segment(k, rows):
    lane = lax.broadcasted_iota(jnp.int32, (rows, KV_DIM), 1)
    return (lane >= k * HEAD_DIM) & (lane < (k + 1) * HEAD_DIM)


def _attn_specs(layer):
    return [_layer_spec((1, D_MODEL), layer),
            _layer_spec((D_MODEL, D_MODEL), layer),
            _layer_spec((1, KV_DIM), layer),
            _layer_spec((D_MODEL, D_MODEL), layer)]


def _attn_prompt_kernel(slopes_ref, sinks_ref, x_ref, kp_ref, kc_ref, vp_ref, vc_ref, g_ref, wq_ref, qg_ref, wo_ref,
                        o_ref, bias_sc, attn_sc, *, nt, layer):
    i = pl.program_id(0)
    t = i % nt
    delta = (BLOCK + lax.broadcasted_iota(jnp.int32, (BLOCK, 2 * BLOCK), 0)
             - lax.broadcasted_iota(jnp.int32, (BLOCK, 2 * BLOCK), 1))
    band = (delta >= 0) & (delta <= WINDOW)

    @pl.when(i == 0)
    def _():
        df = delta.astype(F32)
        for hd in range(N_HEADS):
            bias_sc[hd] = jnp.where(band, -slopes_ref[hd] * df, -jnp.inf)

    x = x_ref[...]
    seg = _segment_ones()
    qs = _queries(x, g_ref[...], wq_ref, qg_ref[...], seg)
    kcur = kc_ref[...]
    vcur = vc_ref[...]
    col = lax.broadcasted_iota(jnp.int32, (GROUP * BLOCK, 2 * BLOCK), 1)
    no_prev = jnp.where((col < BLOCK) & (t == 0), -jnp.inf, 0.0)

    for qb in range(TM // BLOCK):
        rows = slice(qb * BLOCK, (qb + 1) * BLOCK)
        if qb == 0:
            kctx = jnp.concatenate([kp_ref[...], kcur[:BLOCK]], axis=0)
            vctx = jnp.concatenate([vp_ref[...], vcur[:BLOCK]], axis=0)
        else:
            kctx = kcur[(qb - 1) * BLOCK:(qb + 1) * BLOCK]
            vctx = vcur[(qb - 1) * BLOCK:(qb + 1) * BLOCK]
        for k in range(N_KV_HEADS):
            seg_k = _lane_segment(k, BLOCK)
            qm = jnp.concatenate([jnp.where(seg_k, qs[gi][rows], 0.0) for gi in range(GROUP)], axis=0)
            s = _dot_nt(qm.astype(BF16), kctx)
            s = s + bias_sc[k * GROUP:(k + 1) * GROUP].reshape(GROUP * BLOCK, 2 * BLOCK)
            if qb == 0:
                s = s + no_prev
            ps, inv = [], []
            for gi in range(GROUP):
                sg = s[gi * BLOCK:(gi + 1) * BLOCK]
                sink = sinks_ref[layer, k * GROUP + gi]
                m = jnp.maximum(jnp.max(sg, axis=-1, keepdims=True), sink)
                p = jnp.exp(sg - m)
                inv.append(1.0 / (jnp.sum(p, axis=-1, keepdims=True) + jnp.exp(sink - m)))
                ps.append(p.astype(BF16))
            pv = _dot(jnp.concatenate(ps, axis=0), vctx)
            for gi in range(GROUP):
                og = pv[gi * BLOCK:(gi + 1) * BLOCK] * inv[gi]
                lanes = slice(gi * KV_DIM + k * HEAD_DIM, gi * KV_DIM + (k + 1) * HEAD_DIM)
                attn_sc[rows, lanes] = og[:, k * HEAD_DIM:(k + 1) * HEAD_DIM]

    o_ref[...] = x + _dot(attn_sc[...].astype(BF16), wo_ref[...])


def _attn_prompt(x, batch, kb, vb, slopes, sinks, layer, weights):
    rows = x.shape[0]
    nt = rows // batch // TM
    per = TM // BLOCK
    prev_spec = pl.BlockSpec((BLOCK, KV_DIM), lambda i: (jnp.maximum(i * per - 1, 0), 0))
    cur_spec = pl.BlockSpec((TM, KV_DIM), lambda i: (i, 0))
    smem = pl.BlockSpec(memory_space=pltpu.SMEM)
    return pl.pallas_call(
        functools.partial(_attn_prompt_kernel, nt=nt, layer=layer),
        grid=(rows // TM,),
        in_specs=[smem, smem,
                  pl.BlockSpec((TM, D_MODEL), lambda i: (i, 0)),
                  prev_spec, cur_spec, prev_spec, cur_spec] + _attn_specs(layer),
        out_specs=pl.BlockSpec((TM, D_MODEL), lambda i: (i, 0)),
        out_shape=jax.ShapeDtypeStruct((rows, D_MODEL), F32),
        scratch_shapes=[pltpu.VMEM((N_HEADS, BLOCK, 2 * BLOCK), F32),
                        pltpu.VMEM((TM, D_MODEL), F32)],
        compiler_params=_params(),
        name="attn_prompt",
    )(slopes, sinks, x, kb, kb, vb, vb, *weights)


def _attn_sample_kernel(slopes_ref, sinks_ref, x_ref, ks_ref, vs_ref, kn_ref, vn_ref, g_ref, wq_ref, qg_ref, wo_ref,
                        o_ref, qx_sc, ox_sc, *, nsteps, buf_len, layer):
    i = pl.program_id(0)
    n = x_ref.shape[0]

    @pl.when(i == 0)
    def _():
        qs = _queries(x_ref[...], g_ref[...], wq_ref, qg_ref[...], _segment_ones())
        for k in range(N_KV_HEADS):
            seg_k = _lane_segment(k, n)
            for gi in range(GROUP):
                qx_sc[:, k * GROUP + gi, :] = jnp.where(seg_k, qs[gi], 0.0)

    hrow = lax.broadcasted_iota(jnp.int32, (N_HEADS, 1), 0)
    slope_col = jnp.zeros((N_HEADS, 1), F32)
    sink_col = jnp.zeros((N_HEADS, 1), F32)
    for hd in range(N_HEADS):
        slope_col = jnp.where(hrow == hd, slopes_ref[hd], slope_col)
        sink_col = jnp.where(hrow == hd, sinks_ref[layer, hd], sink_col)
    dist = (buf_len - lax.broadcasted_iota(jnp.int32, (N_HEADS, buf_len), 1)).astype(F32)
    bias = -slope_col * dist
    head_seg = (lax.broadcasted_iota(jnp.int32, (N_HEADS, KV_DIM), 0) // GROUP
                == lax.broadcasted_iota(jnp.int32, (N_HEADS, KV_DIM), 1) // HEAD_DIM)

    def one_sample(j, carry):
        b = i * SAMPLE_BLOCK + j
        qx = qx_sc[b].astype(BF16)
        kn = kn_ref[pl.ds(b, 1), :].astype(BF16).astype(F32)
        vn = vn_ref[pl.ds(b, 1), :].astype(BF16).astype(F32)
        s = _dot(qx, ks_ref[j].astype(BF16)) + bias
        s_new = jnp.sum(qx.astype(F32) * kn, axis=-1, keepdims=True)
        m = jnp.maximum(jnp.maximum(jnp.max(s, axis=-1, keepdims=True), s_new), sink_col)
        p = jnp.exp(s - m)
        p_new = jnp.exp(s_new - m)
        denom = jnp.sum(p, axis=-1, keepdims=True) + p_new + jnp.exp(sink_col - m)
        pv = _dot_nt(p.astype(BF16), vs_ref[j].astype(BF16)) + p_new.astype(BF16).astype(F32) * vn
        ox_sc[b] = jnp.where(head_seg, pv / denom, 0.0)
        return carry

    lax.fori_loop(0, SAMPLE_BLOCK, one_sample, 0, unroll=4)

    @pl.when(i == nsteps - 1)
    def _():
        groups = []
        for gi in range(GROUP):
            og = ox_sc[:, gi, :]
            for k in range(1, N_KV_HEADS):
                og = og + ox_sc[:, k * GROUP + gi, :]
            groups.append(og.astype(BF16))
        o_ref[...] = x_ref[...] + _dot(jnp.concatenate(groups, axis=1), wo_ref[...])


def _attn_sample(x, k_state, v_state, k_new, v_new, slopes, sinks, layer, weights):
    n, buf_len = k_state.shape[0], k_state.shape[2]
    nsteps = n // SAMPLE_BLOCK
    st_spec = pl.BlockSpec((SAMPLE_BLOCK, KV_DIM, buf_len), lambda i: (i, 0, 0))
    smem = pl.BlockSpec(memory_space=pltpu.SMEM)
    return pl.pallas_call(
        functools.partial(_attn_sample_kernel, nsteps=nsteps, buf_len=buf_len, layer=layer),
        grid=(nsteps,),
        in_specs=[smem, smem,
                  _const_spec((n, D_MODEL)),
                  st_spec, st_spec,
                  _const_spec((n, KV_DIM)),
                  _const_spec((n, KV_DIM))] + _attn_specs(layer),
        out_specs=pl.BlockSpec((n, D_MODEL), lambda i: (0, 0)),
        out_shape=jax.ShapeDtypeStruct((n, D_MODEL), F32),
        scratch_shapes=[pltpu.VMEM((n, N_HEADS, KV_DIM), F32),
                        pltpu.VMEM((n, N_HEADS, KV_DIM), F32)],
        compiler_params=_params(),
        name="attn_sample",
    )(slopes, sinks, x, k_state, v_state, k_new, v_new, *weights)


def _window_shift_kernel(ks_ref, vs_ref, kn_ref, vn_ref, ko_ref, vo_ref, knt_sc, vnt_sc, *, buf_len):
    i = pl.program_id(0)

    @pl.when(i == 0)
    def _():
        knt_sc[...] = kn_ref[...].T
        vnt_sc[...] = vn_ref[...].T

    newest = lax.broadcasted_iota(jnp.int32, (KV_DIM, buf_len), 1) == buf_len - 1

    def one_sample(j, carry):
        b = i * SAMPLE_BLOCK + j
        for s_ref, nt_sc, out_ref in ((ks_ref, knt_sc, ko_ref), (vs_ref, vnt_sc, vo_ref)):
            shifted = pltpu.roll(s_ref[j], buf_len - 1, 1)
            new_col = pltpu.roll(nt_sc[...], buf_len - 1 - b, 1)
            out_ref[j] = jnp.where(newest, new_col, shifted)
        return carry

    lax.fori_loop(0, SAMPLE_BLOCK, one_sample, 0)


def _window_shift(k_state, v_state, k_new, v_new):
    n, buf_len = k_state.shape[0], k_state.shape[2]
    assert n == buf_len
    st_spec = pl.BlockSpec((SAMPLE_BLOCK, KV_DIM, buf_len), lambda i: (i, 0, 0))
    new_spec = _const_spec((n, KV_DIM))
    return pl.pallas_call(
        functools.partial(_window_shift_kernel, buf_len=buf_len),
        grid=(n // SAMPLE_BLOCK,),
        in_specs=[st_spec, st_spec, new_spec, new_spec],
        out_specs=[st_spec, st_spec],
        out_shape=[jax.ShapeDtypeStruct(k_state.shape, F32)] * 2,
        scratch_shapes=[pltpu.VMEM((KV_DIM, n), F32)] * 2,
        compiler_params=_params(),
        name="window_shift",
    )(k_state, v_state, k_new, v_new)


def _regroup_cols(w):
    n = w.shape[0]
    w = w.reshape(n, D_MODEL, N_KV_HEADS, GROUP, HEAD_DIM).transpose(0, 1, 3, 2, 4)
    return w.reshape(n, D_MODEL, D_MODEL)


def _regroup_rows(w):
    n = w.shape[0]
    w = w.reshape(n, N_KV_HEADS, GROUP, HEAD_DIM, D_MODEL).transpose(0, 2, 1, 3, 4)
    return w.reshape(n, D_MODEL, D_MODEL)


def kernel(x_prompt, x_sample, state_conv, state_win_k, state_win_v, norm_mix_g, norm_mlp_g, conv_w1, conv_b1, conv_dw, conv_dwb, conv_ln_g, conv_ln_b, conv_w2, conv_b2, kv_norm_g, w_kv, k_norm_g, attn_wq, q_norm_g, attn_sinks, attn_wo, mlp_w1, mlp_w2):
    bp, sp, _ = x_prompt.shape
    bs = x_sample.shape[0]
    n_a = conv_w1.shape[0]
    depth = mlp_w1.shape[0]
    buf_len = state_win_k.shape[1]
    assert x_sample.shape[1] == 1 and sp % TM == 0 and buf_len == WINDOW and bs % SAMPLE_BLOCK == 0

    xp = x_prompt.reshape(bp * sp, D_MODEL)
    xs = x_sample.reshape(bs, D_MODEL)
    rows3 = lambda v: v.reshape(v.shape[0], 1, v.shape[-1])
    slopes = jnp.exp2(-8.0 * jnp.arange(1, N_HEADS + 1, dtype=F32) / N_HEADS)
    head_gain = lambda gvec: jnp.tile(gvec, (1, N_KV_HEADS)).reshape(gvec.shape[0], 1, KV_DIM)

    mix_g, mlp_g = rows3(norm_mix_g), rows3(norm_mlp_g)
    conv_weights = (mix_g, conv_w1.astype(BF16), rows3(conv_b1), conv_dw, rows3(conv_dwb),
                    rows3(conv_ln_g), rows3(conv_ln_b), conv_w2.astype(BF16), rows3(conv_b2))
    dw_slabs = conv_dw.astype(BF16).reshape(n_a, CONV_WIDTH, D_MODEL // LANES, LANES).transpose(0, 2, 1, 3)
    dw_packed = jnp.broadcast_to(dw_slabs[:, :, :, None, :], (n_a, D_MODEL // LANES, CONV_WIDTH, PACKED_ROWS, LANES))
    mlp_weights = (mlp_g, mlp_w1.astype(BF16), mlp_w2.astype(BF16))
    kv_weights = (kv_norm_g.reshape(1, D_MODEL), w_kv.astype(BF16), jnp.tile(k_norm_g, N_KV_HEADS).reshape(1, KV_DIM))
    attn_weights = (mix_g[n_a:], _regroup_cols(attn_wq).astype(BF16), head_gain(q_norm_g),
                    _regroup_rows(attn_wo).astype(BF16))

    conv_state = state_conv.transpose(0, 2, 1, 3)
    k_state = state_win_k.transpose(0, 2, 3, 1).reshape(bs, KV_DIM, buf_len)
    v_state = state_win_v.transpose(0, 2, 3, 1).reshape(bs, KV_DIM, buf_len)

    conv_p, conv_u = [], []
    for l in range(depth):
        if l < n_a:
            xp, carry = _amix_prompt(xp, bp, l, conv_weights, dw_packed)
            xs, u = _amix_sample(xs, conv_state, l, conv_weights)
            conv_p.append(carry[:, CARRY - CONV_PREV:, :])
            conv_u.append(u)
        else:
            if l == n_a:
                kb, vb, kf, vf = _kv_proj(xp, bp, *kv_weights, tm=TM, keep=min(WINDOW, sp))
                _, _, kn, vn = _kv_proj(xs, 1, *kv_weights, tm=bs, keep=bs)
                new_ks, new_vs = _window_shift(k_state, v_state, kn, vn)
            bl = l - n_a
            xp = _attn_prompt(xp, bp, kb, vb, slopes, attn_sinks, bl, attn_weights)
            xs = _attn_sample(xs, k_state, v_state, kn, vn, slopes, attn_sinks, bl, attn_weights)
        xs, xp = _mlp_both(xs, xp, l, *mlp_weights)

    new_conv_s = _conv_state(conv_state, jnp.stack(conv_u, axis=0)).transpose(0, 2, 1, 3)
    keep = min(WINDOW, sp)
    kv_shape = (N_KV_HEADS, HEAD_DIM)
    window_out = lambda w: w.reshape(bs, N_KV_HEADS, HEAD_DIM, buf_len).transpose(0, 3, 1, 2)
    return (xp.reshape(bp, sp, D_MODEL), xs.reshape(bs, 1, D_MODEL),
            jnp.stack(conv_p, axis=0), new_conv_s,
            kf.reshape(bp, keep, *kv_shape), vf.reshape(bp, keep, *kv_shape),
            window_out(new_ks), window_out(new_vs))
```

```python
import functools

import jax
import jax.numpy as jnp
from jax import lax
from jax.experimental import pallas as pl
from jax.experimental.pallas import tpu as pltpu

F32 = jnp.float32
BF16 = jnp.bfloat16

D_MODEL = 1024
N_HEADS = 16
HEAD_DIM = 64
N_KV_HEADS = 4
GROUP = 4
KV_DIM = N_KV_HEADS * HEAD_DIM
WINDOW = 128
BLOCK = 128
CONV_WIDTH = 31
CONV_PREV = CONV_WIDTH - 1
CARRY = 32
D_FF = 4 * D_MODEL
RMS_EPS = 1e-6
LN_EPS = 1e-5

SUBLANES = 8
PACKED_ROWS = 16
LANES = 128
TM = 512
MLP_TM = 1024
FF_CHUNK = 1024
STATE_COLS = 256
SAMPLE_BLOCK = 32
VMEM_LIMIT = 56 * 1024 * 1024


def _const_spec(shape):
    nd = len(shape)
    return pl.BlockSpec(shape, lambda *_: (0,) * nd, pipeline_mode=pl.Buffered(1))


def _layer_spec(shape, layer):
    nd = len(shape)
    return pl.BlockSpec((None,) + tuple(shape), lambda *_: (layer,) + (0,) * nd, pipeline_mode=pl.Buffered(1))


def _params(n_axes=1):
    return pltpu.CompilerParams(dimension_semantics=("arbitrary",) * n_axes, vmem_limit_bytes=VMEM_LIMIT)


def _rms(x, g):
    return x * lax.rsqrt(jnp.mean(x * x, axis=-1, keepdims=True) + RMS_EPS) * g


def _dot(a, b):
    return jnp.dot(a, b, preferred_element_type=F32)


def _dot_nt(a, b):
    return lax.dot_general(a, b, (((1,), (1,)), ((), ())), preferred_element_type=F32)


def _segment_ones():
    r = lax.broadcasted_iota(jnp.int32, (KV_DIM, KV_DIM), 0) // HEAD_DIM
    c = lax.broadcasted_iota(jnp.int32, (KV_DIM, KV_DIM), 1) // HEAD_DIM
    return (r == c).astype(BF16)


def _head_rms(z, g, seg):
    s = z * z
    hi = s.astype(BF16)
    lo = (s - hi.astype(F32)).astype(BF16)
    ms = (_dot(hi, seg) + _dot(lo, seg)) * (1.0 / HEAD_DIM)
    return z * lax.rsqrt(ms + RMS_EPS) * g


def _mlp_rows(x_ref, g_ref, w1_ref, w2_ref, o_ref):
    x = x_ref[...]
    h = _rms(x, g_ref[...]).astype(BF16)
    acc = x
    for c in range(D_FF // FF_CHUNK):
        a = _dot(h, w1_ref[:, c * FF_CHUNK:(c + 1) * FF_CHUNK])
        a = jnp.maximum(a, 0.0)
        a = (a * a).astype(BF16)
        acc = acc + _dot(a, w2_ref[c * FF_CHUNK:(c + 1) * FF_CHUNK, :])
    o_ref[...] = acc
    return acc


def _shared_kv(x, g_ref, w_ref, kg_ref):
    h = _rms(x, g_ref[...]).astype(BF16)
    kv = _dot(h, w_ref[...])
    return _head_rms(kv[:, :KV_DIM], kg_ref[...], _segment_ones()), kv[:, KV_DIM:]


def _mlp_both_kernel(*refs, with_kv):
    xs_ref, xp_ref, g_ref, w1_ref, w2_ref = refs[:5]
    if with_kv:
        kvg_ref, wkv_ref, kg_ref, os_ref, op_ref, kn_ref, vn_ref, kb_ref, vb_ref, kf_ref, vf_ref = refs[5:]
    else:
        os_ref, op_ref = refs[5:]
    i = pl.program_id(0)

    @pl.when(i == 0)
    def _():
        out = _mlp_rows(xs_ref, g_ref, w1_ref, w2_ref, os_ref)
        if with_kv:
            kn_ref[...], vn_ref[...] = _shared_kv(out, kvg_ref, wkv_ref, kg_ref)

    @pl.when(i > 0)
    def _():
        out = _mlp_rows(xp_ref, g_ref, w1_ref, w2_ref, op_ref)
        if with_kv:
            k, v = _shared_kv(out, kvg_ref, wkv_ref, kg_ref)
            kb_ref[...] = k.astype(BF16)
            vb_ref[...] = v.astype(BF16)
            keep = kf_ref.shape[0]
            kf_ref[...] = k[MLP_TM - keep:, :]
            vf_ref[...] = v[MLP_TM - keep:, :]


def _mlp_both(xs, xp, layer, g, w1, w2, kv_weights=None, batch=None, keep=None):
    n, rows = xs.shape[0], xp.shape[0]
    with_kv = kv_weights is not None
    prompt_tile = lambda i: (jnp.maximum(i - 1, 0), 0)
    in_specs = [_const_spec((n, D_MODEL)),
                pl.BlockSpec((MLP_TM, D_MODEL), prompt_tile),
                _layer_spec((1, D_MODEL), layer),
                _layer_spec((D_MODEL, D_FF), layer),
                _layer_spec((D_FF, D_MODEL), layer)]
    out_specs = [pl.BlockSpec((n, D_MODEL), lambda i: (0, 0)),
                 pl.BlockSpec((MLP_TM, D_MODEL), prompt_tile)]
    out_shape = [jax.ShapeDtypeStruct((n, D_MODEL), F32),
                 jax.ShapeDtypeStruct((rows, D_MODEL), F32)]
    args = (xs, xp, g, w1, w2)
    if with_kv:
        nt = rows // batch // MLP_TM
        in_specs += [_const_spec((1, D_MODEL)), _const_spec((D_MODEL, 2 * KV_DIM)), _const_spec((1, KV_DIM))]
        new_spec = pl.BlockSpec((n, KV_DIM), lambda i: (0, 0))
        row_spec = pl.BlockSpec((MLP_TM, KV_DIM), prompt_tile)
        keep_spec = pl.BlockSpec((keep, KV_DIM), lambda i: (jnp.maximum(i - 1, 0) // nt, 0))
        out_specs += [new_spec, new_spec, row_spec, row_spec, keep_spec, keep_spec]
        out_shape += [jax.ShapeDtypeStruct((n, KV_DIM), F32)] * 2
        out_shape += [jax.ShapeDtypeStruct((rows, KV_DIM), BF16)] * 2
        out_shape += [jax.ShapeDtypeStruct((batch * keep, KV_DIM), F32)] * 2
        args += tuple(kv_weights)
    return pl.pallas_call(
        functools.partial(_mlp_both_kernel, with_kv=with_kv),
        grid=(1 + rows // MLP_TM,),
        in_specs=in_specs,
        out_specs=out_specs,
        out_shape=out_shape,
        compiler_params=_params(),
        name="mlp_both",
    )(*args)


def _glu_in(x, g, w1_ref, b1_ref):
    h = _rms(x, g).astype(BF16)
    a = _dot(h, w1_ref[...]) + b1_ref[...]
    return a[:, :D_MODEL] * jax.nn.sigmoid(a[:, D_MODEL:])


def _conv_out(x, c, lng, lnb, w2_ref, b2_ref):
    mu = jnp.mean(c, axis=-1, keepdims=True)
    d = c - mu
    var = jnp.mean(d * d, axis=-1, keepdims=True)
    y = d * lax.rsqrt(var + LN_EPS) * lng + lnb
    y = y * jax.nn.sigmoid(y)
    return x + _dot(y.astype(BF16), w2_ref[...]) + b2_ref[...]


def _amix_specs(layer):
    return [_layer_spec((1, D_MODEL), layer),
            _layer_spec((D_MODEL, 2 * D_MODEL), layer),
            _layer_spec((1, 2 * D_MODEL), layer),
            _layer_spec((CONV_WIDTH, D_MODEL), layer),
            _layer_spec((1, D_MODEL), layer),
            _layer_spec((1, D_MODEL), layer),
            _layer_spec((1, D_MODEL), layer),
            _layer_spec((D_MODEL, D_MODEL), layer),
            _layer_spec((1, D_MODEL), layer)]


def _amix_prompt_kernel(x_ref, g_ref, w1_ref, b1_ref, dw_ref, dwb_ref, lng_ref, lnb_ref, w2_ref, b2_ref, dwp_ref,
                        o_ref, carry_ref, ubuf, esh, cbuf, *, nt):
    del dw_ref
    t = pl.program_id(0) % nt

    @pl.when(t == 0)
    def _():
        ubuf[0:CARRY, :] = jnp.zeros((CARRY, D_MODEL), F32)

    x = x_ref[...]
    ubuf[CARRY:CARRY + TM, :] = _glu_in(x, g_ref[...], w1_ref, b1_ref)

    first = CARRY - CONV_PREV
    shifted = TM + PACKED_ROWS
    for c in range(D_MODEL // LANES):
        cols = slice(c * LANES, (c + 1) * LANES)
        slab = ubuf[:, cols]
        esh[0] = slab.astype(BF16)
        esh[SUBLANES, 0:shifted] = slab[SUBLANES:SUBLANES + shifted].astype(BF16)
        for b in range(1, SUBLANES):
            sb = pltpu.roll(slab, CARRY + TM - b, 0)
            esh[b, 0:shifted] = sb[0:shifted].astype(BF16)
            esh[b + SUBLANES, 0:shifted] = sb[SUBLANES:SUBLANES + shifted].astype(BF16)

        @pl.when(t >= 0)
        def _():
            for r0 in range(0, TM, PACKED_ROWS):
                acc = jnp.zeros((PACKED_ROWS, LANES), F32)
                for o in range(first, CARRY + 1):
                    a, b = divmod(o, PACKED_ROWS)
                    src = r0 + a * PACKED_ROWS
                    acc = acc + esh[b, src:src + PACKED_ROWS, :].astype(F32) * dwp_ref[c, o - first].astype(F32)
                cbuf[r0:r0 + PACKED_ROWS, cols] = acc + dwb_ref[:, cols]

    tail = ubuf[TM:TM + CARRY, :]
    carry_ref[...] = tail
    ubuf[0:CARRY, :] = tail

    o_ref[...] = _conv_out(x, cbuf[...], lng_ref[...], lnb_ref[...], w2_ref, b2_ref)


def _amix_prompt(x, batch, layer, weights, dw_packed):
    rows = x.shape[0]
    nt = rows // batch // TM
    return pl.pallas_call(
        functools.partial(_amix_prompt_kernel, nt=nt),
        grid=(rows // TM,),
        in_specs=([pl.BlockSpec((TM, D_MODEL), lambda i: (i, 0))] + _amix_specs(layer)
                  + [_layer_spec((D_MODEL // LANES, CONV_WIDTH, PACKED_ROWS, LANES), layer)]),
        out_specs=[pl.BlockSpec((TM, D_MODEL), lambda i: (i, 0)),
                   pl.BlockSpec((None, CARRY, D_MODEL), lambda i: (i // nt, 0, 0))],
        out_shape=[jax.ShapeDtypeStruct((rows, D_MODEL), F32),
                   jax.ShapeDtypeStruct((batch, CARRY, D_MODEL), F32)],
        scratch_shapes=[pltpu.VMEM((CARRY + TM, D_MODEL), F32),
                        pltpu.VMEM((2 * SUBLANES, CARRY + TM, LANES), BF16),
                        pltpu.VMEM((TM, D_MODEL), F32)],
        compiler_params=_params(),
        name="amix_prompt",
    )(x, *weights, dw_packed)


def _amix_sample_kernel(x_ref, st_ref, g_ref, w1_ref, b1_ref, dw_ref, dwb_ref, lng_ref, lnb_ref, w2_ref, b2_ref,
                        o_ref, u_ref, c_sc, *, nsteps):
    i = pl.program_id(0)

    @pl.when(i == 0)
    def _():
        u_ref[...] = _glu_in(x_ref[...], g_ref[...], w1_ref, b1_ref)

    cols = pl.ds(pl.multiple_of(i * STATE_COLS, STATE_COLS), STATE_COLS)
    acc = dwb_ref[:, cols] + u_ref[:, cols] * dw_ref[CONV_PREV:CONV_WIDTH, cols]
    for j in range(CONV_PREV):
        acc = acc + st_ref[j] * dw_ref[j:j + 1, cols]
    c_sc[:, cols] = acc

    @pl.when(i == nsteps - 1)
    def _():
        o_ref[...] = _conv_out(x_ref[...], c_sc[...], lng_ref[...], lnb_ref[...], w2_ref, b2_ref)


def _amix_sample(x, state, layer, weights):
    n = x.shape[0]
    nsteps = D_MODEL // STATE_COLS
    return pl.pallas_call(
        functools.partial(_amix_sample_kernel, nsteps=nsteps),
        grid=(nsteps,),
        in_specs=[_const_spec((n, D_MODEL)),
                  pl.BlockSpec((None, CONV_PREV, n, STATE_COLS), lambda i: (layer, 0, 0, i))] + _amix_specs(layer),
        out_specs=[pl.BlockSpec((n, D_MODEL), lambda i: (0, 0)),
                   pl.BlockSpec((n, D_MODEL), lambda i: (0, 0))],
        out_shape=[jax.ShapeDtypeStruct((n, D_MODEL), F32),
                   jax.ShapeDtypeStruct((n, D_MODEL), F32)],
        scratch_shapes=[pltpu.VMEM((n, D_MODEL), F32)],
        compiler_params=_params(),
        name="amix_sample",
    )(x, state, *weights)


def _conv_state_kernel(st_ref, u_ref, o_ref):
    for j in range(1, CONV_PREV):
        o_ref[j - 1] = st_ref[j]
    o_ref[CONV_PREV - 1] = u_ref[...]


def _conv_state(state, u):
    n_layers, _, n, _ = state.shape
    st_spec = pl.BlockSpec((None, CONV_PREV, n, STATE_COLS), lambda a, c: (a, 0, 0, c))
    return pl.pallas_call(
        _conv_state_kernel,
        grid=(n_layers, D_MODEL // STATE_COLS),
        in_specs=[st_spec, pl.BlockSpec((None, n, STATE_COLS), lambda a, c: (a, 0, c))],
        out_specs=st_spec,
        out_shape=jax.ShapeDtypeStruct(state.shape, F32),
        compiler_params=_params(2),
        name="conv_state",
    )(state, u)


def _queries(x, g, wq_ref, qg, seg):
    h = _rms(x, g).astype(BF16)
    q = _dot(h, wq_ref[...])
    out = []
    for gi in range(GROUP):
        qn = _head_rms(q[:, gi * KV_DIM:(gi + 1) * KV_DIM], qg, seg)
        out.append(qn * (HEAD_DIM ** -0.5))
    return out


def _lane_segment(k, rows):
    lane = lax.broadcasted_iota(jnp.int32, (rows, KV_DIM), 1)
    return (lane >= k * HEAD_DIM) & (lane < (k + 1) * HEAD_DIM)


def _attn_specs(layer):
    return [_layer_spec((1, D_MODEL), layer),
            _layer_spec((D_MODEL, D_MODEL), layer),
            _layer_spec((1, KV_DIM), layer),
            _layer_spec((D_MODEL, D_MODEL), layer)]


def _attn_prompt_kernel(slopes_ref, sinks_ref, x_ref, kp_ref, kc_ref, vp_ref, vc_ref, g_ref, wq_ref, qg_ref, wo_ref,
                        o_ref, bias_sc, attn_sc, *, nt, layer):
    i = pl.program_id(0)
    t = i % nt
    delta = (BLOCK + lax.broadcasted_iota(jnp.int32, (BLOCK, 2 * BLOCK), 0)
             - lax.broadcasted_iota(jnp.int32, (BLOCK, 2 * BLOCK), 1))
    band = (delta >= 0) & (delta <= WINDOW)

    @pl.when(i == 0)
    def _():
        df = delta.astype(F32)
        for hd in range(N_HEADS):
            bias_sc[hd] = jnp.where(band, -slopes_ref[hd] * df, -jnp.inf)

    x = x_ref[...]
    seg = _segment_ones()
    qs = _queries(x, g_ref[...], wq_ref, qg_ref[...], seg)
    kcur = kc_ref[...]
    vcur = vc_ref[...]
    col = lax.broadcasted_iota(jnp.int32, (GROUP * BLOCK, 2 * BLOCK), 1)
    no_prev = jnp.where((col < BLOCK) & (t == 0), -jnp.inf, 0.0)

    for qb in range(TM // BLOCK):
        rows = slice(qb * BLOCK, (qb + 1) * BLOCK)
        if qb == 0:
            kctx = jnp.concatenate([kp_ref[...], kcur[:BLOCK]], axis=0)
            vctx = jnp.concatenate([vp_ref[...], vcur[:BLOCK]], axis=0)
        else:
            kctx = kcur[(qb - 1) * BLOCK:(qb + 1) * BLOCK]
            vctx = vcur[(qb - 1) * BLOCK:(qb + 1) * BLOCK]
        for k in range(N_KV_HEADS):
            seg_k = _lane_segment(k, BLOCK)
            qm = jnp.concatenate([jnp.where(seg_k, qs[gi][rows], 0.0) for gi in range(GROUP)], axis=0)
            s = _dot_nt(qm.astype(BF16), kctx)
            s = s + bias_sc[k * GROUP:(k + 1) * GROUP].reshape(GROUP * BLOCK, 2 * BLOCK)
            if qb == 0:
                s = s + no_prev
            ps, inv = [], []
            for gi in range(GROUP):
                sg = s[gi * BLOCK:(gi + 1) * BLOCK]
                sink = sinks_ref[layer, k * GROUP + gi]
                m = jnp.maximum(jnp.max(sg, axis=-1, keepdims=True), sink)
                p = jnp.exp(sg - m)
                inv.append(1.0 / (jnp.sum(p, axis=-1, keepdims=True) + jnp.exp(sink - m)))
                ps.append(p.astype(BF16))
            pv = _dot(jnp.concatenate(ps, axis=0), vctx)
            for gi in range(GROUP):
                og = pv[gi * BLOCK:(gi + 1) * BLOCK] * inv[gi]
                lanes = slice(gi * KV_DIM + k * HEAD_DIM, gi * KV_DIM + (k + 1) * HEAD_DIM)
                attn_sc[rows, lanes] = og[:, k * HEAD_DIM:(k + 1) * HEAD_DIM]

    o_ref[...] = x + _dot(attn_sc[...].astype(BF16), wo_ref[...])


def _attn_prompt(x, batch, kb, vb, slopes, sinks, layer, weights):
    rows = x.shape[0]
    nt = rows // batch // TM
    per = TM // BLOCK
    prev_spec = pl.BlockSpec((BLOCK, KV_DIM), lambda i: (jnp.maximum(i * per - 1, 0), 0))
    cur_spec = pl.BlockSpec((TM, KV_DIM), lambda i: (i, 0))
    smem = pl.BlockSpec(memory_space=pltpu.SMEM)
    return pl.pallas_call(
        functools.partial(_attn_prompt_kernel, nt=nt, layer=layer),
        grid=(rows // TM,),
        in_specs=[smem, smem,
                  pl.BlockSpec((TM, D_MODEL), lambda i: (i, 0)),
                  prev_spec, cur_spec, prev_spec, cur_spec] + _attn_specs(layer),
        out_specs=pl.BlockSpec((TM, D_MODEL), lambda i: (i, 0)),
        out_shape=jax.ShapeDtypeStruct((rows, D_MODEL), F32),
        scratch_shapes=[pltpu.VMEM((N_HEADS, BLOCK, 2 * BLOCK), F32),
                        pltpu.VMEM((TM, D_MODEL), F32)],
        compiler_params=_params(),
        name="attn_prompt",
    )(slopes, sinks, x, kb, kb, vb, vb, *weights)


def _attn_sample_kernel(slopes_ref, sinks_ref, x_ref, ks_ref, vs_ref, kn_ref, vn_ref, g_ref, wq_ref, qg_ref, wo_ref,
                        o_ref, qx_sc, ox_sc, *, nsteps, buf_len, layer):
    i = pl.program_id(0)
    n = x_ref.shape[0]

    @pl.when(i == 0)
    def _():
        qs = _queries(x_ref[...], g_ref[...], wq_ref, qg_ref[...], _segment_ones())
        for k in range(N_KV_HEADS):
            seg_k = _lane_segment(k, n)
            for gi in range(GROUP):
                qx_sc[:, k * GROUP + gi, :] = jnp.where(seg_k, qs[gi], 0.0)

    hrow = lax.broadcasted_iota(jnp.int32, (N_HEADS, 1), 0)
    slope_col = jnp.zeros((N_HEADS, 1), F32)
    sink_col = jnp.zeros((N_HEADS, 1), F32)
    for hd in range(N_HEADS):
        slope_col = jnp.where(hrow == hd, slopes_ref[hd], slope_col)
        sink_col = jnp.where(hrow == hd, sinks_ref[layer, hd], sink_col)
    dist = (buf_len - lax.broadcasted_iota(jnp.int32, (N_HEADS, buf_len), 1)).astype(F32)
    bias = -slope_col * dist
    head_seg = (lax.broadcasted_iota(jnp.int32, (N_HEADS, KV_DIM), 0) // GROUP
                == lax.broadcasted_iota(jnp.int32, (N_HEADS, KV_DIM), 1) // HEAD_DIM)

    def one_sample(j, carry):
        b = i * SAMPLE_BLOCK + j
        qx = qx_sc[b].astype(BF16)
        kn = kn_ref[pl.ds(b, 1), :].astype(BF16).astype(F32)
        vn = vn_ref[pl.ds(b, 1), :].astype(BF16).astype(F32)
        s = _dot(qx, ks_ref[j].astype(BF16)) + bias
        s_new = jnp.sum(qx.astype(F32) * kn, axis=-1, keepdims=True)
        m = jnp.maximum(jnp.maximum(jnp.max(s, axis=-1, keepdims=True), s_new), sink_col)
        p = jnp.exp(s - m)
        p_new = jnp.exp(s_new - m)
        denom = jnp.sum(p, axis=-1, keepdims=True) + p_new + jnp.exp(sink_col - m)
        pv = _dot_nt(p.astype(BF16), vs_ref[j].astype(BF16)) + p_new.astype(BF16).astype(F32) * vn
        ox_sc[b] = jnp.where(head_seg, pv / denom, 0.0)
        return carry

    lax.fori_loop(0, SAMPLE_BLOCK, one_sample, 0, unroll=4)

    @pl.when(i == nsteps - 1)
    def _():
        groups = []
        for gi in range(GROUP):
            og = ox_sc[:, gi, :]
            for k in range(1, N_KV_HEADS):
                og = og + ox_sc[:, k * GROUP + gi, :]
            groups.append(og.astype(BF16))
        o_ref[...] = x_ref[...] + _dot(jnp.concatenate(groups, axis=1), wo_ref[...])


def _attn_sample(x, k_state, v_state, k_new, v_new, slopes, sinks, layer, weights):
    n, buf_len = k_state.shape[0], k_state.shape[2]
    nsteps = n // SAMPLE_BLOCK
    st_spec = pl.BlockSpec((SAMPLE_BLOCK, KV_DIM, buf_len), lambda i: (i, 0, 0))
    smem = pl.BlockSpec(memory_space=pltpu.SMEM)
    return pl.pallas_call(
        functools.partial(_attn_sample_kernel, nsteps=nsteps, buf_len=buf_len, layer=layer),
        grid=(nsteps,),
        in_specs=[smem, smem,
                  _const_spec((n, D_MODEL)),
                  st_spec, st_spec,
                  _const_spec((n, KV_DIM)),
                  _const_spec((n, KV_DIM))] + _attn_specs(layer),
        out_specs=pl.BlockSpec((n, D_MODEL), lambda i: (0, 0)),
        out_shape=jax.ShapeDtypeStruct((n, D_MODEL), F32),
        scratch_shapes=[pltpu.VMEM((n, N_HEADS, KV_DIM), F32),
                        pltpu.VMEM((n, N_HEADS, KV_DIM), F32)],
        compiler_params=_params(),
        name="attn_sample",
    )(slopes, sinks, x, k_state, v_state, k_new, v_new, *weights)


def _window_shift_kernel(ks_ref, vs_ref, kn_ref, vn_ref, ko_ref, vo_ref, knt_sc, vnt_sc, *, buf_len):
    i = pl.program_id(0)

    @pl.when(i == 0)
    def _():
        knt_sc[...] = kn_ref[...].T
        vnt_sc[...] = vn_ref[...].T

    newest = lax.broadcasted_iota(jnp.int32, (KV_DIM, buf_len), 1) == buf_len - 1

    def one_sample(j, carry):
        b = i * SAMPLE_BLOCK + j
        for s_ref, nt_sc, out_ref in ((ks_ref, knt_sc, ko_ref), (vs_ref, vnt_sc, vo_ref)):
            shifted = pltpu.roll(s_ref[j], buf_len - 1, 1)
            new_col = pltpu.roll(nt_sc[...], buf_len - 1 - b, 1)
            out_ref[j] = jnp.where(newest, new_col, shifted)
        return carry

    lax.fori_loop(0, SAMPLE_BLOCK, one_sample, 0)


def _window_shift(k_state, v_state, k_new, v_new):
    n, buf_len = k_state.shape[0], k_state.shape[2]
    assert n == buf_len
    st_spec = pl.BlockSpec((SAMPLE_BLOCK, KV_DIM, buf_len), lambda i: (i, 0, 0))
    new_spec = _const_spec((n, KV_DIM))
    return pl.pallas_call(
        functools.partial(_window_shift_kernel, buf_len=buf_len),
        grid=(n // SAMPLE_BLOCK,),
        in_specs=[st_spec, st_spec, new_spec, new_spec],
        out_specs=[st_spec, st_spec],
        out_shape=[jax.ShapeDtypeStruct(k_state.shape, F32)] * 2,
        scratch_shapes=[pltpu.VMEM((KV_DIM, n), F32)] * 2,
        compiler_params=_params(),
        name="window_shift",
    )(k_state, v_state, k_new, v_new)


def _regroup_cols(w):
    n = w.shape[0]
    w = w.reshape(n, D_MODEL, N_KV_HEADS, GROUP, HEAD_DIM).transpose(0, 1, 3, 2, 4)
    return w.reshape(n, D_MODEL, D_MODEL)


def _regroup_rows(w):
    n = w.shape[0]
    w = w.reshape(n, N_KV_HEADS, GROUP, HEAD_DIM, D_MODEL).transpose(0, 2, 1, 3, 4)
    return w.reshape(n, D_MODEL, D_MODEL)


def kernel(x_prompt, x_sample, state_conv, state_win_k, state_win_v, norm_mix_g, norm_mlp_g, conv_w1, conv_b1, conv_dw, conv_dwb, conv_ln_g, conv_ln_b, conv_w2, conv_b2, kv_norm_g, w_kv, k_norm_g, attn_wq, q_norm_g, attn_sinks, attn_wo, mlp_w1, mlp_w2):
    bp, sp, _ = x_prompt.shape
    bs = x_sample.shape[0]
    n_a = conv_w1.shape[0]
    depth = mlp_w1.shape[0]
    buf_len = state_win_k.shape[1]
    assert x_sample.shape[1] == 1 and sp % TM == 0 and buf_len == WINDOW and bs % SAMPLE_BLOCK == 0

    xp = x_prompt.reshape(bp * sp, D_MODEL)
    xs = x_sample.reshape(bs, D_MODEL)
    rows3 = lambda v: v.reshape(v.shape[0], 1, v.shape[-1])
    slopes = jnp.exp2(-8.0 * jnp.arange(1, N_HEADS + 1, dtype=F32) / N_HEADS)
    head_gain = lambda gvec: jnp.tile(gvec, (1, N_KV_HEADS)).reshape(gvec.shape[0], 1, KV_DIM)

    mix_g, mlp_g = rows3(norm_mix_g), rows3(norm_mlp_g)
    conv_weights = (mix_g, conv_w1.astype(BF16), rows3(conv_b1), conv_dw, rows3(conv_dwb),
                    rows3(conv_ln_g), rows3(conv_ln_b), conv_w2.astype(BF16), rows3(conv_b2))
    dw_slabs = conv_dw.astype(BF16).reshape(n_a, CONV_WIDTH, D_MODEL // LANES, LANES).transpose(0, 2, 1, 3)
    dw_packed = jnp.broadcast_to(dw_slabs[:, :, :, None, :], (n_a, D_MODEL // LANES, CONV_WIDTH, PACKED_ROWS, LANES))
    mlp_weights = (mlp_g, mlp_w1.astype(BF16), mlp_w2.astype(BF16))
    kv_weights = (kv_norm_g.reshape(1, D_MODEL), w_kv.astype(BF16), jnp.tile(k_norm_g, N_KV_HEADS).reshape(1, KV_DIM))
    attn_weights = (mix_g[n_a:], _regroup_cols(attn_wq).astype(BF16), head_gain(q_norm_g),
                    _regroup_rows(attn_wo).astype(BF16))

    conv_state = state_conv.transpose(0, 2, 1, 3)
    k_state = state_win_k.transpose(0, 2, 3, 1).reshape(bs, KV_DIM, buf_len)
    v_state = state_win_v.transpose(0, 2, 3, 1).reshape(bs, KV_DIM, buf_len)

    conv_p, conv_u = [], []
    for l in range(depth):
        if l < n_a:
            xp, carry = _amix_prompt(xp, bp, l, conv_weights, dw_packed)
            xs, u = _amix_sample(xs, conv_state, l, conv_weights)
            conv_p.append(carry[:, CARRY - CONV_PREV:, :])
            conv_u.append(u)
        else:
            if l == n_a:
                new_ks, new_vs = _window_shift(k_state, v_state, kn, vn)
            bl = l - n_a
            xp = _attn_prompt(xp, bp, kb, vb, slopes, attn_sinks, bl, attn_weights)
            xs = _attn_sample(xs, k_state, v_state, kn, vn, slopes, attn_sinks, bl, attn_weights)
        if l == n_a - 1:
            xs, xp, kn, vn, kb, vb, kf, vf = _mlp_both(xs, xp, l, *mlp_weights, kv_weights=kv_weights,
                                                       batch=bp, keep=min(WINDOW, sp))
        else:
            xs, xp = _mlp_both(xs, xp, l, *mlp_weights)

    new_conv_s = _conv_state(conv_state, jnp.stack(conv_u, axis=0)).transpose(0, 2, 1, 3)
    keep = min(WINDOW, sp)
    kv_shape = (N_KV_HEADS, HEAD_DIM)
    window_out = lambda w: w.reshape(bs, N_KV_HEADS, HEAD_DIM, buf_len).transpose(0, 3, 1, 2)
    return (xp.reshape(bp, sp, D_MODEL), xs.reshape(bs, 1, D_MODEL),
            jnp.stack(conv_p, axis=0), new_conv_s,
            kf.reshape(bp, keep, *kv_shape), vf.reshape(bp, keep, *kv_shape),
            window_out(new_ks), window_out(new_vs))
```
